```python
import math
import jax, jax.numpy as jnp
from jax import lax
import numpy as np

D_MODEL = 1024
BATCH = 2
SEQ = 8192
DEPTH = 4

MLA_HEADS = 4
MLA_NOPE_DIM = 128
MLA_ROPE_DIM = 64
MLA_V_DIM = 128
MLA_Q_RANK = 256
MLA_KV_RANK = 128
DIFF_HEADS = 4
DIFF_HEAD_DIM = 64
MIX_WIDTH = MLA_HEADS * MLA_V_DIM + DIFF_HEADS * 2 * DIFF_HEAD_DIM
IN_SPLITS = (MLA_Q_RANK, MLA_KV_RANK, MLA_ROPE_DIM,
             DIFF_HEADS * 2 * DIFF_HEAD_DIM,
             DIFF_HEADS * 2 * DIFF_HEAD_DIM,
             DIFF_HEADS * 2 * DIFF_HEAD_DIM)
IN_WIDTH = sum(IN_SPLITS)
D_FF = 2816
Q_BLOCK = 128
ROPE_THETA = 10000.0
NORM_EPS = 1e-5
DEEPNORM_ALPHA = (2 * DEPTH) ** 0.25
DEEPNORM_BETA = (8 * DEPTH) ** -0.25

kernel_name = "hymba_mla_diffattn_macaron_deepnorm"


def layer_norm(x, g, b):
    xf = x.astype(jnp.float32)
    mu = jnp.mean(xf, axis=-1, keepdims=True)
    var = jnp.mean(jnp.square(xf - mu), axis=-1, keepdims=True)
    y = (xf - mu) * lax.rsqrt(var + NORM_EPS)
    return (y * g.astype(jnp.float32) + b.astype(jnp.float32)).astype(x.dtype)


def rms_norm(x, g):
    xf = x.astype(jnp.float32)
    y = xf * lax.rsqrt(jnp.mean(jnp.square(xf), axis=-1, keepdims=True) + NORM_EPS)
    return (y * g.astype(jnp.float32)).astype(x.dtype)


def swiglu(x, w_gate, w_up, w_down):
    return (jax.nn.silu(x @ w_gate) * (x @ w_up)) @ w_down


def rope_tables(pos, dim):
    inv_freq = ROPE_THETA ** (-jnp.arange(0, dim, 2, dtype=jnp.float32) / dim)
    ang = pos.astype(jnp.float32)[:, None] * inv_freq[None, :]
    return jnp.cos(ang), jnp.sin(ang)


def apply_rope(x, cos, sin):
    x1, x2 = jnp.split(x, 2, axis=-1)
    cos = cos.astype(x.dtype)
    sin = sin.astype(x.dtype)
    return jnp.concatenate([x1 * cos - x2 * sin, x2 * cos + x1 * sin], axis=-1)


def sweep_query_blocks(block_fn, *q_side):
    b, s = q_side[0].shape[:2]
    nb = s // Q_BLOCK
    blocks = tuple(jnp.swapaxes(a.reshape((b, nb, Q_BLOCK) + a.shape[2:]), 0, 1) for a in q_side)
    out = lax.map(lambda blk: block_fn(*blk), blocks)
    out = jnp.swapaxes(out, 0, 1)
    return out.reshape((b, s) + out.shape[3:])


def mla_mixer(c_q, c_kv, k_rope_raw, pos, q_norm_g, w_q_up, kv_norm_g, w_kv_up):
    b, s, _ = c_q.shape
    q = (rms_norm(c_q, q_norm_g) @ w_q_up).reshape(b, s, MLA_HEADS, MLA_NOPE_DIM + MLA_ROPE_DIM)
    q_nope, q_rope = q[..., :MLA_NOPE_DIM], q[..., MLA_NOPE_DIM:]
    kv = (rms_norm(c_kv, kv_norm_g) @ w_kv_up).reshape(b, s, MLA_HEADS, MLA_NOPE_DIM + MLA_V_DIM)
    k_nope, v = kv[..., :MLA_NOPE_DIM], kv[..., MLA_NOPE_DIM:]
    cos, sin = rope_tables(pos, MLA_ROPE_DIM)
    q_rope = apply_rope(q_rope, cos[:, None, :], sin[:, None, :])
    k_rope = apply_rope(k_rope_raw, cos, sin)
    q = jnp.concatenate([q_nope, q_rope], axis=-1)
    k = jnp.concatenate([k_nope, jnp.broadcast_to(k_rope[:, :, None, :], (b, s, MLA_HEADS, MLA_ROPE_DIM))], axis=-1)
    scale = (MLA_NOPE_DIM + MLA_ROPE_DIM) ** -0.5

    def block(qb):
        sc = jnp.einsum('bqhd,bkhd->bhqk', qb, k).astype(jnp.float32) * scale
        p = jax.nn.softmax(sc, axis=-1).astype(v.dtype)
        return jnp.einsum('bhqk,bkhd->bqhd', p, v)

    o = sweep_query_blocks(block, q)
    return o.reshape(b, s, MLA_HEADS * MLA_V_DIM)


def diff_mixer(q, k, v, pos, lq1, lk1, lq2, lk2, subln_g, lambda_init):
    b, s, _ = q.shape
    q = q.reshape(b, s, DIFF_HEADS, 2, DIFF_HEAD_DIM)
    k = k.reshape(b, s, DIFF_HEADS, 2, DIFF_HEAD_DIM)
    v = v.reshape(b, s, DIFF_HEADS, 2 * DIFF_HEAD_DIM)
    q1, q2 = q[..., 0, :], q[..., 1, :]
    k1, k2 = k[..., 0, :], k[..., 1, :]
    lam = (jnp.exp(jnp.sum(lq1.astype(jnp.float32) * lk1.astype(jnp.float32)))
           - jnp.exp(jnp.sum(lq2.astype(jnp.float32) * lk2.astype(jnp.float32)))
           + lambda_init)
    slopes = 2.0 ** (-8.0 * jnp.arange(1, DIFF_HEADS + 1, dtype=jnp.float32) / DIFF_HEADS)
    scale = DIFF_HEAD_DIM ** -0.5
    pos_q = jnp.broadcast_to(pos[None, :], (b, s))

    def block(q1b, q2b, pb):
        dist = jnp.abs(pb[:, :, None] - pos[None, None, :]).astype(jnp.float32)
        bias = -slopes[None, :, None, None] * dist[:, None, :, :]
        s1 = jnp.einsum('bqhd,bkhd->bhqk', q1b, k1).astype(jnp.float32) * scale + bias
        s2 = jnp.einsum('bqhd,bkhd->bhqk', q2b, k2).astype(jnp.float32) * scale + bias
        a = (jax.nn.softmax(s1, axis=-1) - lam * jax.nn.softmax(s2, axis=-1)).astype(v.dtype)
        return jnp.einsum('bhqk,bkhd->bqhd', a, v)

    o = sweep_query_blocks(block, q1, q2, pos_q)
    o = rms_norm(o, subln_g) * (1.0 - lambda_init)
    return o.reshape(b, s, DIFF_HEADS * 2 * DIFF_HEAD_DIM)


def setup_inputs(seed: int = 0) -> dict:
    key = jax.random.key(seed)
    ks = jax.random.split(key, 32)
    L, D, F = DEPTH, D_MODEL, D_FF
    f32 = jnp.float32

    def w(k, shape, fan_in, gain=1.0):
        return jax.random.normal(k, shape, f32) * (gain * fan_in ** -0.5)

    def gain(k, shape):
        return 1.0 + 0.02 * jax.random.normal(k, shape, f32)

    def bias(k, shape):
        return 0.02 * jax.random.normal(k, shape, f32)

    return {
        "x": jax.random.normal(ks[0], (BATCH, SEQ, D), f32),
        "ffn1_w_gate": w(ks[1], (L, D, F), D),
        "ffn1_w_up": w(ks[2], (L, D, F), D),
        "ffn1_w_down": w(ks[3], (L, F, D), F, DEEPNORM_BETA),
        "ln1_g": gain(ks[4], (L, D)),
        "ln1_b": bias(ks[5], (L, D)),
        "w_in": w(ks[6], (L, D, IN_WIDTH), D),
        "q_norm_g": gain(ks[7], (L, MLA_Q_RANK)),
        "w_q_up": w(ks[8], (L, MLA_Q_RANK, MLA_HEADS * (MLA_NOPE_DIM + MLA_ROPE_DIM)), MLA_Q_RANK),
        "kv_norm_g": gain(ks[9], (L, MLA_KV_RANK)),
        "w_kv_up": w(ks[10], (L, MLA_KV_RANK, MLA_HEADS * (MLA_NOPE_DIM + MLA_V_DIM)), MLA_KV_RANK),
        "diff_lambda_q1": 0.1 * jax.random.normal(ks[11], (L, DIFF_HEAD_DIM), f32),
        "diff_lambda_k1": 0.1 * jax.random.normal(ks[12], (L, DIFF_HEAD_DIM), f32),
        "diff_lambda_q2": 0.1 * jax.random.normal(ks[13], (L, DIFF_HEAD_DIM), f32),
        "diff_lambda_k2": 0.1 * jax.random.normal(ks[14], (L, DIFF_HEAD_DIM), f32),
        "diff_subln_g": gain(ks[15], (L, 2 * DIFF_HEAD_DIM)),
        "w_out": w(ks[16], (L, MIX_WIDTH, D), MIX_WIDTH, DEEPNORM_BETA),
        "ln2_g": gain(ks[17], (L, D)),
        "ln2_b": bias(ks[18], (L, D)),
        "ffn2_w_gate": w(ks[19], (L, D, F), D),
        "ffn2_w_up": w(ks[20], (L, D, F), D),
        "ffn2_w_down": w(ks[21], (L, F, D), F, DEEPNORM_BETA),
        "ln3_g": gain(ks[22], (L, D)),
        "ln3_b": bias(ks[23], (L, D)),
    }


def reference(x, ffn1_w_gate, ffn1_w_up, ffn1_w_down, ln1_g, ln1_b, w_in, q_norm_g, w_q_up,
              kv_norm_g, w_kv_up, diff_lambda_q1, diff_lambda_k1, diff_lambda_q2, diff_lambda_k2,
              diff_subln_g, w_out, ln2_g, ln2_b, ffn2_w_gate, ffn2_w_up, ffn2_w_down, ln3_g, ln3_b):
    s = x.shape[1]
    pos = jnp.arange(s, dtype=jnp.int32)
    split_idx = [int(i) for i in np.cumsum(IN_SPLITS)[:-1]]
    for l in range(DEPTH):
        lambda_init = 0.8 - 0.6 * math.exp(-0.3 * l)
        x = layer_norm(DEEPNORM_ALPHA * x + 0.5 * swiglu(x, ffn1_w_gate[l], ffn1_w_up[l], ffn1_w_down[l]),
                       ln1_g[l], ln1_b[l])
        h = x @ w_in[l]
        c_q, c_kv, k_rope, dq, dk, dv = jnp.split(h, split_idx, axis=-1)
        o_mla = mla_mixer(c_q, c_kv, k_rope, pos, q_norm_g[l], w_q_up[l], kv_norm_g[l], w_kv_up[l])
        o_diff = diff_mixer(dq, dk, dv, pos, diff_lambda_q1[l], diff_lambda_k1[l], diff_lambda_q2[l],
                            diff_lambda_k2[l], diff_subln_g[l], lambda_init)
        mix = jnp.concatenate([o_mla, o_diff], axis=-1) @ w_out[l]
        x = layer_norm(DEEPNORM_ALPHA * x + mix, ln2_g[l], ln2_b[l])
        x = layer_norm(DEEPNORM_ALPHA * x + 0.5 * swiglu(x, ffn2_w_gate[l], ffn2_w_up[l], ffn2_w_down[l]),
                       ln3_g[l], ln3_b[l])
    return x
```

```python
import functools
import math

import jax
import jax.numpy as jnp
import numpy as np
from jax import lax
from jax.experimental import pallas as pl
from jax.experimental.pallas import tpu as pltpu

D_MODEL = 1024
DEPTH = 4
MLA_HEADS = 4
MLA_NOPE = 128
MLA_ROPE = 64
MLA_V = 128
MLA_Q_RANK = 256
MLA_KV_RANK = 128
DIFF_HEADS = 4
DIFF_DIM = 64
D_FF = 2816
ROPE_THETA = 10000.0
NORM_EPS = 1e-5
ALPHA = (2 * DEPTH) ** 0.25

LANE = 128
ROW_TILE = 512
KV_CHUNK = ROW_TILE
Q_TILE = 512
FF_TILE = 1408
POS_SPLIT = 128
VMEM_LIMIT = 48 * 1024 * 1024

MLA_QK = MLA_NOPE + 2 * MLA_ROPE
W_IN_COLS = 2048
NEG_INF = float("-inf")


def _dot(a, b):
    return jnp.dot(a, b, preferred_element_type=jnp.float32)


def _layer_norm(y, g, b):
    mu = jnp.mean(y, axis=-1, keepdims=True)
    d = y - mu
    var = jnp.mean(d * d, axis=-1, keepdims=True)
    return d * lax.rsqrt(var + NORM_EPS) * g + b


def _rms_norm(y, g):
    return y * lax.rsqrt(jnp.mean(y * y, axis=-1, keepdims=True) + NORM_EPS) * g


def _ffn_ln_kernel(x_ref, wg_ref, wu_ref, wd_ref, g_ref, b_ref, o_ref, xb_sc, acc_sc):
    f = pl.program_id(1)

    @pl.when(f == 0)
    def _():
        xb_sc[...] = x_ref[...].astype(jnp.bfloat16)
        acc_sc[...] = jnp.zeros_like(acc_sc)

    xb = xb_sc[...]
    gate = _dot(xb, wg_ref[...])
    up = _dot(xb, wu_ref[...])
    h = gate / (1.0 + jnp.exp(-gate)) * up
    acc_sc[...] += _dot(h.astype(jnp.bfloat16), wd_ref[...])

    @pl.when(f == pl.num_programs(1) - 1)
    def _():
        y = ALPHA * x_ref[...] + 0.5 * acc_sc[...]
        o_ref[...] = _layer_norm(y, g_ref[...], b_ref[...])


def _ffn_ln(x, wg, wu, wd, g, b):
    t = x.shape[0]
    nf = D_FF // FF_TILE
    return pl.pallas_call(
        _ffn_ln_kernel,
        grid=(t // ROW_TILE, nf),
        in_specs=[
            pl.BlockSpec((ROW_TILE, D_MODEL), lambda i, f: (i, 0)),
            pl.BlockSpec((D_MODEL, FF_TILE), lambda i, f: (0, f)),
            pl.BlockSpec((D_MODEL, FF_TILE), lambda i, f: (0, f)),
            pl.BlockSpec((FF_TILE, D_MODEL), lambda i, f: (f, 0)),
            pl.BlockSpec((1, D_MODEL), lambda i, f: (0, 0)),
            pl.BlockSpec((1, D_MODEL), lambda i, f: (0, 0)),
        ],
        out_specs=pl.BlockSpec((ROW_TILE, D_MODEL), lambda i, f: (i, 0)),
        out_shape=jax.ShapeDtypeStruct((t, D_MODEL), jnp.float32),
        scratch_shapes=[
            pltpu.VMEM((ROW_TILE, D_MODEL), jnp.bfloat16),
            pltpu.VMEM((ROW_TILE, D_MODEL), jnp.float32),
        ],
        compiler_params=pltpu.CompilerParams(
            dimension_semantics=("parallel", "arbitrary"), vmem_limit_bytes=VMEM_LIMIT),
        name="ffn_ln",
    )(x, wg, wu, wd, g, b)


def _alibi_slope(h):
    return 2.0 ** (-8.0 * (h + 1) / DIFF_HEADS)


def _inproj_kernel(x_ref, win_ref, qg_ref, wq_ref, kvg_ref, wkv_ref, rope_ref,
                   qt_ref, km_ref, vtm_ref, q1t_ref, q2t_ref, k1_ref, k2_ref, vtd_ref):
    i = pl.program_id(1)
    tm = x_ref.shape[1]
    xb = x_ref[0].astype(jnp.bfloat16)
    h = _dot(xb, win_ref[...])
    c_q = h[:, 0:256]
    c_kv = h[:, 256:384]
    kr = h[:, 384:512]
    dq = h[:, 512:1024]
    dk = h[:, 1024:1536]
    dv = h[:, 1536:2048]
    rope = rope_ref[...]

    q = _dot(_rms_norm(c_q, qg_ref[...]).astype(jnp.bfloat16), wq_ref[...])
    kv = _dot(_rms_norm(c_kv, kvg_ref[...]).astype(jnp.bfloat16), wkv_ref[...])
    krt = kr * rope
    k_rope = krt + pltpu.roll(krt, MLA_ROPE, axis=1)
    mla_scale = (MLA_NOPE + MLA_ROPE) ** -0.5
    for hh in range(MLA_HEADS):
        qh = q[:, hh * MLA_QK:(hh + 1) * MLA_QK]
        qh = jnp.concatenate([qh[:, :MLA_NOPE], qh[:, MLA_NOPE:] * rope], axis=1) * mla_scale
        qt_ref[0, hh] = qh.T.astype(jnp.bfloat16)
        kvh = kv[:, hh * 256:(hh + 1) * 256]
        km_ref[0, hh] = jnp.concatenate([kvh[:, :MLA_NOPE], k_rope], axis=1).astype(jnp.bfloat16)
        vtm_ref[0, hh, 0] = kvh[:, MLA_NOPE:].T.astype(jnp.bfloat16)

    pos_c = i * tm + lax.broadcasted_iota(jnp.int32, (8, tm), 1)
    row = lax.broadcasted_iota(jnp.int32, (8, tm), 0)
    hi_c = (pos_c & -POS_SPLIT).astype(jnp.float32)
    lo_c = (pos_c & (POS_SPLIT - 1)).astype(jnp.float32)
    pos_r = i * tm + lax.broadcasted_iota(jnp.int32, (tm, LANE), 0)
    lane = lax.broadcasted_iota(jnp.int32, (tm, LANE), 1)
    hi_r = (pos_r & -POS_SPLIT).astype(jnp.float32)
    lo_r = (pos_r & (POS_SPLIT - 1)).astype(jnp.float32)
    diff_scale = DIFF_DIM ** -0.5
    zpad = jnp.zeros((DIFF_DIM - 8, tm), jnp.float32)
    for hh in range(DIFF_HEADS):
        slope = _alibi_slope(hh)
        qaug = jnp.where(row == 0, -slope * hi_c,
                         jnp.where(row == 1, -slope * lo_c,
                                   jnp.where(row < 4, 1.0, 0.0)))
        kaug = jnp.where(lane < DIFF_DIM + 2, 1.0,
                         jnp.where(lane == DIFF_DIM + 2, slope * hi_r,
                                   jnp.where(lane == DIFF_DIM + 3, slope * lo_r, 0.0)))
        tq = (dq[:, hh * 128:(hh + 1) * 128] * diff_scale).T
        q1t_ref[0, hh] = jnp.concatenate([tq[:DIFF_DIM], qaug, zpad], axis=0).astype(jnp.bfloat16)
        q2t_ref[0, hh] = jnp.concatenate([tq[DIFF_DIM:], qaug, zpad], axis=0).astype(jnp.bfloat16)
        tk = dk[:, hh * 128:(hh + 1) * 128]
        k1_ref[0, hh] = jnp.where(lane < DIFF_DIM, tk, kaug).astype(jnp.bfloat16)
        k2_ref[0, hh] = jnp.where(lane < DIFF_DIM, pltpu.roll(tk, DIFF_DIM, axis=1), kaug).astype(jnp.bfloat16)
        vtd_ref[0, hh, 0] = dv[:, hh * 128:(hh + 1) * 128].T.astype(jnp.bfloat16)


def _inproj(x, win, qg, wq, kvg, wkv, rope):
    b, s, _ = x.shape
    nc = s // KV_CHUNK
    tm = ROW_TILE
    bf = jnp.bfloat16
    const = lambda shape: pl.BlockSpec(shape, lambda bb, i: (0,) * len(shape))
    out_shape = (
        jax.ShapeDtypeStruct((b, MLA_HEADS, MLA_QK, s), bf),
        jax.ShapeDtypeStruct((b, MLA_HEADS, s, MLA_QK), bf),
        jax.ShapeDtypeStruct((b, MLA_HEADS, nc, MLA_V, KV_CHUNK), bf),
        jax.ShapeDtypeStruct((b, DIFF_HEADS, LANE, s), bf),
        jax.ShapeDtypeStruct((b, DIFF_HEADS, LANE, s), bf),
        jax.ShapeDtypeStruct((b, DIFF_HEADS, s, LANE), bf),
        jax.ShapeDtypeStruct((b, DIFF_HEADS, s, LANE), bf),
        jax.ShapeDtypeStruct((b, DIFF_HEADS, nc, 2 * DIFF_DIM, KV_CHUNK), bf),
    )
    out_specs = (
        pl.BlockSpec((1, MLA_HEADS, MLA_QK, tm), lambda bb, i: (bb, 0, 0, i)),
        pl.BlockSpec((1, MLA_HEADS, tm, MLA_QK), lambda bb, i: (bb, 0, i, 0)),
        pl.BlockSpec((1, MLA_HEADS, 1, MLA_V, tm), lambda bb, i: (bb, 0, i, 0, 0)),
        pl.BlockSpec((1, DIFF_HEADS, LANE, tm), lambda bb, i: (bb, 0, 0, i)),
        pl.BlockSpec((1, DIFF_HEADS, LANE, tm), lambda bb, i: (bb, 0, 0, i)),
        pl.BlockSpec((1, DIFF_HEADS, tm, LANE), lambda bb, i: (bb, 0, i, 0)),
        pl.BlockSpec((1, DIFF_HEADS, tm, LANE), lambda bb, i: (bb, 0, i, 0)),
        pl.BlockSpec((1, DIFF_HEADS, 1, 2 * DIFF_DIM, tm), lambda bb, i: (bb, 0, i, 0, 0)),
    )
    return pl.pallas_call(
        _inproj_kernel,
        grid=(b, s // tm),
        in_specs=[
            pl.BlockSpec((1, tm, D_MODEL), lambda bb, i: (bb, i, 0)),
            const((D_MODEL, W_IN_COLS)),
            const((1, MLA_Q_RANK)),
            const((MLA_Q_RANK, MLA_HEADS * MLA_QK)),
            const((1, MLA_KV_RANK)),
            const((MLA_KV_RANK, MLA_HEADS * 256)),
            pl.BlockSpec((tm, LANE), lambda bb, i: (i, 0)),
        ],
        out_specs=out_specs,
        out_shape=out_shape,
        compiler_params=pltpu.CompilerParams(
            dimension_semantics=("parallel", "parallel"), vmem_limit_bytes=VMEM_LIMIT),
        name="inproj",
    )(x, win, qg, wq, kvg, wkv, rope)


def _softmax_chunk(s, vt_ref, c, m_sc, l_sc, acc_sc):
    m_old = m_sc[...]
    m_new = jnp.maximum(m_old, jnp.max(s, axis=0, keepdims=True))
    alpha = jnp.exp(m_old - m_new)
    p = jnp.exp(s - m_new)
    l_sc[...] = alpha * l_sc[...] + jnp.sum(p, axis=0, keepdims=True)
    acc_sc[...] = alpha * acc_sc[...] + _dot(vt_ref[0, 0, c], p.astype(jnp.bfloat16))
    m_sc[...] = m_new


def _mla_kernel(qt_ref, k_ref, vt_ref, o_ref, m_sc, l_sc, acc_sc):
    nk = vt_ref.shape[2]
    m_sc[...] = jnp.full_like(m_sc, NEG_INF)
    l_sc[...] = jnp.zeros_like(l_sc)
    acc_sc[...] = jnp.zeros_like(acc_sc)
    qt = qt_ref[0, 0]

    def body(c, carry):
        kc = k_ref[0, 0, pl.ds(pl.multiple_of(c * KV_CHUNK, KV_CHUNK), KV_CHUNK), :]
        _softmax_chunk(_dot(kc, qt), vt_ref, c, m_sc, l_sc, acc_sc)
        return carry

    lax.fori_loop(0, nk, body, 0)
    o = acc_sc[...] / l_sc[...]
    o_ref[0] = o.T.astype(o_ref.dtype)


def _mla_attention(qt, k, vt):
    b, hds, _, s = qt.shape
    nc = vt.shape[2]
    return pl.pallas_call(
        _mla_kernel,
        grid=(b, hds, s // Q_TILE),
        in_specs=[
            pl.BlockSpec((1, 1, MLA_QK, Q_TILE), lambda bb, h, i: (bb, h, 0, i)),
            pl.BlockSpec((1, 1, s, MLA_QK), lambda bb, h, i: (bb, h, 0, 0)),
            pl.BlockSpec((1, 1, nc, MLA_V, KV_CHUNK), lambda bb, h, i: (bb, h, 0, 0, 0)),
        ],
        out_specs=pl.BlockSpec((1, Q_TILE, MLA_V), lambda bb, h, i: (bb, i, h)),
        out_shape=jax.ShapeDtypeStruct((b, s, hds * MLA_V), jnp.bfloat16),
        scratch_shapes=[
            pltpu.VMEM((1, Q_TILE), jnp.float32),
            pltpu.VMEM((1, Q_TILE), jnp.float32),
            pltpu.VMEM((MLA_V, Q_TILE), jnp.float32),
        ],
        compiler_params=pltpu.CompilerParams(
            dimension_semantics=("parallel", "parallel", "arbitrary"), vmem_limit_bytes=VMEM_LIMIT),
        name="mla_attn",
    )(qt, k, vt)


def _diff_kernel(q1t_ref, q2t_ref, k1_ref, k2_ref, vt_ref, lam_ref, linit_ref, g_ref, o_ref,
                 qv_sc, m_sc, l_sc, acc_sc):
    h = pl.program_id(1)
    qi = pl.program_id(2)
    nk = vt_ref.shape[2]
    tq = q1t_ref.shape[3]
    m_sc[...] = jnp.full_like(m_sc, NEG_INF)
    l_sc[...] = jnp.zeros_like(l_sc)
    acc_sc[...] = jnp.zeros_like(acc_sc)
    row = lax.broadcasted_iota(jnp.int32, (LANE, tq), 0)
    for mi, qref in enumerate((q1t_ref, q2t_ref)):
        qq = qref[0, 0]
        qv_sc[mi] = qq
        qv_sc[2 + mi] = jnp.where(row < DIFF_DIM, qq, -qq)
    k_refs = (k1_ref, k2_ref)

    def update(c, variant, overlap):
        rows = pl.ds(pl.multiple_of(c * KV_CHUNK, KV_CHUNK), KV_CHUNK)
        for mi in range(2):
            s = _dot(k_refs[mi][0, 0, rows, :], qv_sc[variant + mi])
            if overlap:
                jj = c * KV_CHUNK + lax.broadcasted_iota(jnp.int32, s.shape, 0)
                ii = qi * tq + lax.broadcasted_iota(jnp.int32, s.shape, 1)
                expo = jnp.full((1, tq), 127 + 1, jnp.int32) - 2 * (h + 1)
                slope2 = lax.bitcast_convert_type(expo << 23, jnp.float32)
                s = s - slope2 * jnp.maximum(jj - ii, 0).astype(jnp.float32)
            _softmax_chunk(s, vt_ref, c, m_sc.at[mi], l_sc.at[mi], acc_sc.at[mi])

    c_lo = (qi * tq) // KV_CHUNK
    c_hi = ((qi + 1) * tq + KV_CHUNK - 1) // KV_CHUNK

    def before(c, carry):
        update(c, 0, False)
        return carry

    def within(c, carry):
        update(c, 0, True)
        return carry

    def after(c, carry):
        update(c, 2, False)
        return carry

    lax.fori_loop(0, c_lo, before, 0)
    lax.fori_loop(c_lo, c_hi, within, 0)
    lax.fori_loop(c_hi, nk, after, 0)

    lamp = lam_ref[...]
    linit = linit_ref[...]
    lam = (jnp.exp(jnp.sum(lamp[0:1] * lamp[1:2], axis=-1, keepdims=True))
           - jnp.exp(jnp.sum(lamp[2:3] * lamp[3:4], axis=-1, keepdims=True)) + linit)
    o = acc_sc[0] / l_sc[0] - lam * (acc_sc[1] / l_sc[1])
    o = _rms_norm(o.T, g_ref[...]) * (1.0 - linit)
    o_ref[0] = o.astype(o_ref.dtype)


def _diff_attention(q1t, q2t, k1, k2, vt, lamp, linit, subln_g):
    b, hds, _, s = q1t.shape
    nc = vt.shape[2]
    qspec = pl.BlockSpec((1, 1, LANE, Q_TILE), lambda bb, h, i: (bb, h, 0, i))
    kspec = pl.BlockSpec((1, 1, s, LANE), lambda bb, h, i: (bb, h, 0, 0))
    const = lambda shape: pl.BlockSpec(shape, lambda bb, h, i: (0,) * len(shape))
    return pl.pallas_call(
        _diff_kernel,
        grid=(b, hds, s // Q_TILE),
        in_specs=[
            qspec, qspec, kspec, kspec,
            pl.BlockSpec((1, 1, nc, 2 * DIFF_DIM, KV_CHUNK), lambda bb, h, i: (bb, h, 0, 0, 0)),
            const((4, DIFF_DIM)), const((1, 1)), const((1, 2 * DIFF_DIM)),
        ],
        out_specs=pl.BlockSpec((1, Q_TILE, 2 * DIFF_DIM), lambda bb, h, i: (bb, i, h)),
        out_shape=jax.ShapeDtypeStruct((b, s, hds * 2 * DIFF_DIM), jnp.bfloat16),
        scratch_shapes=[
            pltpu.VMEM((4, LANE, Q_TILE), jnp.bfloat16),
            pltpu.VMEM((2, 1, Q_TILE), jnp.float32),
            pltpu.VMEM((2, 1, Q_TILE), jnp.float32),
            pltpu.VMEM((2, 2 * DIFF_DIM, Q_TILE), jnp.float32),
        ],
        compiler_params=pltpu.CompilerParams(
            dimension_semantics=("parallel", "parallel", "arbitrary"), vmem_limit_bytes=VMEM_LIMIT),
        name="diff_attn",
    )(q1t, q2t, k1, k2, vt, lamp, linit, subln_g)


def _outproj_ln_kernel(x_ref, om_ref, od_ref, w_ref, g_ref, b_ref, o_ref):
    nm = om_ref.shape[1]
    mix = _dot(om_ref[...], w_ref[:nm, :]) + _dot(od_ref[...], w_ref[nm:, :])
    o_ref[...] = _layer_norm(ALPHA * x_ref[...] + mix, g_ref[...], b_ref[...])


def _outproj_ln(x, om, od, w, g, b):
    t = x.shape[0]
    nm, nd = om.shape[1], od.shape[1]
    return pl.pallas_call(
        _outproj_ln_kernel,
        grid=(t // ROW_TILE,),
        in_specs=[
            pl.BlockSpec((ROW_TILE, D_MODEL), lambda i: (i, 0)),
            pl.BlockSpec((ROW_TILE, nm), lambda i: (i, 0)),
            pl.BlockSpec((ROW_TILE, nd), lambda i: (i, 0)),
            pl.BlockSpec((nm + nd, D_MODEL), lambda i: (0, 0)),
            pl.BlockSpec((1, D_MODEL), lambda i: (0, 0)),
            pl.BlockSpec((1, D_MODEL), lambda i: (0, 0)),
        ],
        out_specs=pl.BlockSpec((ROW_TILE, D_MODEL), lambda i: (i, 0)),
        out_shape=jax.ShapeDtypeStruct((t, D_MODEL), jnp.float32),
        compiler_params=pltpu.CompilerParams(
            dimension_semantics=("parallel",), vmem_limit_bytes=VMEM_LIMIT),
        name="outproj_ln",
    )(x, om, od, w, g, b)


def _swap_halves(w):
    half = w.shape[-1] // 2
    return jnp.concatenate([w[..., half:], w[..., :half]], axis=-1)


def _prep_w_in(w_in):
    c_q = w_in[..., :256]
    c_kv = w_in[..., 256:384]
    k_rope = w_in[..., 384:448]
    rest = w_in[..., 448:]
    return jnp.concatenate([c_q, c_kv, k_rope, _swap_halves(k_rope), rest], axis=-1).astype(jnp.bfloat16)


def _prep_w_q_up(w_q_up):
    l, r, _ = w_q_up.shape
    w = w_q_up.reshape(l, r, MLA_HEADS, MLA_NOPE + MLA_ROPE)
    rope = w[..., MLA_NOPE:]
    w = jnp.concatenate([w[..., :MLA_NOPE], rope, _swap_halves(rope)], axis=-1)
    return w.reshape(l, r, MLA_HEADS * MLA_QK).astype(jnp.bfloat16)


def _rope_table(s):
    inv_freq = ROPE_THETA ** (-jnp.arange(0, MLA_ROPE, 2, dtype=jnp.float32) / MLA_ROPE)
    ang = jnp.arange(s, dtype=jnp.int32).astype(jnp.float32)[:, None] * inv_freq[None, :]
    cos, sin = jnp.cos(ang), jnp.sin(ang)
    return jnp.concatenate([cos, cos, -sin, sin], axis=-1)


def kernel(x, ffn1_w_gate, ffn1_w_up, ffn1_w_down, ln1_g, ln1_b, w_in, q_norm_g, w_q_up, kv_norm_g, w_kv_up, diff_lambda_q1, diff_lambda_k1, diff_lambda_q2, diff_lambda_k2, diff_subln_g, w_out, ln2_g, ln2_b, ffn2_w_gate, ffn2_w_up, ffn2_w_down, ln3_g, ln3_b):
    b, s, d = x.shape
    bf = jnp.bfloat16
    rope = _rope_table(s)
    w_in_p = _prep_w_in(w_in)
    w_q_p = _prep_w_q_up(w_q_up)
    w_kv_p = w_kv_up.astype(bf)
    w_out_p = w_out.astype(bf)
    f1 = (ffn1_w_gate.astype(bf), ffn1_w_up.astype(bf), ffn1_w_down.astype(bf))
    f2 = (ffn2_w_gate.astype(bf), ffn2_w_up.astype(bf), ffn2_w_down.astype(bf))
    lamp = jnp.stack([diff_lambda_q1, diff_lambda_k1, diff_lambda_q2, diff_lambda_k2], axis=1)
    row = lambda v, l: v[l][None, :]

    xf = x.reshape(b * s, d)
    for l in range(DEPTH):
        linit = jnp.full((1, 1), 0.8 - 0.6 * math.exp(-0.3 * l), jnp.float32)
        xf = _ffn_ln(xf, f1[0][l], f1[1][l], f1[2][l], row(ln1_g, l), row(ln1_b, l))
        qt, km, vtm, q1t, q2t, k1, k2, vtd = _inproj(
            xf.reshape(b, s, d), w_in_p[l], row(q_norm_g, l), w_q_p[l], row(kv_norm_g, l), w_kv_p[l], rope)
        o_mla = _mla_attention(qt, km, vtm)
        o_diff = _diff_attention(q1t, q2t, k1, k2, vtd, lamp[l], linit, row(diff_subln_g, l))
        xf = _outproj_ln(xf, o_mla.reshape(b * s, -1), o_diff.reshape(b * s, -1), w_out_p[l],
                         row(ln2_g, l), row(ln2_b, l))
        xf = _ffn_ln(xf, f2[0][l], f2[1][l], f2[2][l], row(ln3_g, l), row(ln3_b, l))
    return xf.reshape(b, s, d)
```

```python
import math

import jax
import jax.numpy as jnp
import numpy as np
from jax import lax
from jax.experimental import pallas as pl
from jax.experimental.pallas import tpu as pltpu

D_MODEL = 1024
DEPTH = 4
MLA_HEADS = 4
MLA_NOPE = 128
MLA_ROPE = 64
MLA_V = 128
MLA_Q_RANK = 256
MLA_KV_RANK = 128
DIFF_HEADS = 4
DIFF_DIM = 64
D_FF = 2816
ROPE_THETA = 10000.0
NORM_EPS = 1e-5
ALPHA = (2 * DEPTH) ** 0.25

LANE = 128
ROW_TILE = 512
KV_CHUNK = ROW_TILE
DIFF_TILE = 1024
MLA_Q_TILE = 1024
MLA_K_TILE = 1024
FF_TILE = 1408
POS_SPLIT = 128
VMEM_LIMIT = 58 * 1024 * 1024

MLA_QK = MLA_NOPE + 2 * MLA_ROPE
W_IN_COLS = 2048
NEG_INF = float("-inf")
LOG2E = math.log2(math.e)
AUG_ROWS = 16
V_ROWS = MLA_V + AUG_ROWS


def _bf16_pieces(x, n):
    out = []
    for _ in range(n):
        piece = float(np.asarray(x, dtype=jnp.bfloat16).astype(np.float32))
        out.append(piece)
        x -= piece
    return tuple(out)


LOG2E_PIECES = _bf16_pieces(LOG2E, 3)


def _dot(a, b):
    return jnp.dot(a, b, preferred_element_type=jnp.float32)


def _layer_norm(y, g, b):
    mu = jnp.mean(y, axis=-1, keepdims=True)
    d = y - mu
    var = jnp.mean(d * d, axis=-1, keepdims=True)
    return d * lax.rsqrt(var + NORM_EPS) * g + b


def _rms_norm(y, g):
    return y * lax.rsqrt(jnp.mean(y * y, axis=-1, keepdims=True) + NORM_EPS) * g


def _ffn_ln_kernel(x_ref, wg_ref, wu_ref, wd_ref, g_ref, b_ref, o_ref, xb_sc, acc_sc):
    f = pl.program_id(1)

    @pl.when(f == 0)
    def _():
        xb_sc[...] = x_ref[...].astype(jnp.bfloat16)
        acc_sc[...] = jnp.zeros_like(acc_sc)

    xb = xb_sc[...]
    gate = _dot(xb, wg_ref[...])
    up = _dot(xb, wu_ref[...])
    h = gate / (1.0 + jnp.exp(-gate)) * up
    acc_sc[...] += _dot(h.astype(jnp.bfloat16), wd_ref[...])

    @pl.when(f == pl.num_programs(1) - 1)
    def _():
        y = ALPHA * x_ref[...] + 0.5 * acc_sc[...]
        o_ref[...] = _layer_norm(y, g_ref[...], b_ref[...])


def _ffn_ln(x, wg, wu, wd, g, b):
    t = x.shape[0]
    nf = D_FF // FF_TILE
    return pl.pallas_call(
        _ffn_ln_kernel,
        grid=(t // ROW_TILE, nf),
        in_specs=[
            pl.BlockSpec((ROW_TILE, D_MODEL), lambda i, f: (i, 0)),
            pl.BlockSpec((D_MODEL, FF_TILE), lambda i, f: (0, f)),
            pl.BlockSpec((D_MODEL, FF_TILE), lambda i, f: (0, f)),
            pl.BlockSpec((FF_TILE, D_MODEL), lambda i, f: (f, 0)),
            pl.BlockSpec((1, D_MODEL), lambda i, f: (0, 0)),
            pl.BlockSpec((1, D_MODEL), lambda i, f: (0, 0)),
        ],
        out_specs=pl.BlockSpec((ROW_TILE, D_MODEL), lambda i, f: (i, 0)),
        out_shape=jax.ShapeDtypeStruct((t, D_MODEL), jnp.float32),
        scratch_shapes=[
            pltpu.VMEM((ROW_TILE, D_MODEL), jnp.bfloat16),
            pltpu.VMEM((ROW_TILE, D_MODEL), jnp.float32),
        ],
        compiler_params=pltpu.CompilerParams(
            dimension_semantics=("parallel", "arbitrary"), vmem_limit_bytes=VMEM_LIMIT),
        name="ffn_ln",
    )(x, wg, wu, wd, g, b)


def _alibi_slope(h):
    return 2.0 ** (-8.0 * (h + 1) / DIFF_HEADS)


def _inproj_kernel(x_ref, win_ref, qg_ref, wq_ref, kvg_ref, wkv_ref, rope_ref,
                   qt_ref, km_ref, vtm_ref, q1t_ref, q2t_ref, k1_ref, k2_ref, vtd_ref):
    i = pl.program_id(1)
    tm = x_ref.shape[1]
    xb = x_ref[0].astype(jnp.bfloat16)
    h = _dot(xb, win_ref[...])
    c_q = h[:, 0:256]
    c_kv = h[:, 256:384]
    kr = h[:, 384:512]
    dq = h[:, 512:1024]
    dk = h[:, 1024:1536]
    dv = h[:, 1536:2048]
    rope = rope_ref[...]

    q = _dot(_rms_norm(c_q, qg_ref[...]).astype(jnp.bfloat16), wq_ref[...])
    kv = _dot(_rms_norm(c_kv, kvg_ref[...]).astype(jnp.bfloat16), wkv_ref[...])
    krt = kr * rope
    k_rope = krt + pltpu.roll(krt, MLA_ROPE, axis=1)
    mla_scale = (MLA_NOPE + MLA_ROPE) ** -0.5 * LOG2E
    ones_rows = jnp.where(lax.broadcasted_iota(jnp.int32, (AUG_ROWS, tm), 0) == 0, 1.0, 0.0)
    for hh in range(MLA_HEADS):
        qh = q[:, hh * MLA_QK:(hh + 1) * MLA_QK]
        qh = jnp.concatenate([qh[:, :MLA_NOPE], qh[:, MLA_NOPE:] * rope], axis=1) * mla_scale
        qt_ref[0, hh] = qh.T.astype(jnp.bfloat16)
        kvh = kv[:, hh * 256:(hh + 1) * 256]
        km_ref[0, hh] = jnp.concatenate([kvh[:, :MLA_NOPE], k_rope], axis=1).astype(jnp.bfloat16)
        vtm_ref[0, hh, 0] = jnp.concatenate([kvh[:, MLA_NOPE:].T, ones_rows], axis=0).astype(jnp.bfloat16)

    pos_c = i * tm + lax.broadcasted_iota(jnp.int32, (AUG_ROWS, tm), 1)
    row = lax.broadcasted_iota(jnp.int32, (AUG_ROWS, tm), 0)
    hi_c = (pos_c & -POS_SPLIT).astype(jnp.float32)
    lo_c = (pos_c & (POS_SPLIT - 1)).astype(jnp.float32)
    pos_r = i * tm + lax.broadcasted_iota(jnp.int32, (tm, LANE), 0)
    lane = lax.broadcasted_iota(jnp.int32, (tm, LANE), 1)
    hi_r = (pos_r & -POS_SPLIT).astype(jnp.float32)
    lo_r = (pos_r & (POS_SPLIT - 1)).astype(jnp.float32)
    diff_scale = DIFF_DIM ** -0.5 * LOG2E
    zpad = jnp.zeros((DIFF_DIM - AUG_ROWS, tm), jnp.float32)
    def piece(slot):
        return jnp.where(slot < 4, LOG2E_PIECES[0], jnp.where(slot < 8, LOG2E_PIECES[1], LOG2E_PIECES[2]))

    slot_r = lane - DIFF_DIM
    piece_c, term_c = piece(row), row & 3
    piece_r, term_r = piece(slot_r), slot_r & 3
    for hh in range(DIFF_HEADS):
        slope = _alibi_slope(hh)
        qaug = jnp.where(row >= 12, 0.0,
                         jnp.where(term_c == 0, -slope * hi_c,
                                   jnp.where(term_c == 1, -slope * lo_c, piece_c)))
        kaug = jnp.where(slot_r >= 12, 0.0,
                         jnp.where(term_r < 2, piece_r,
                                   jnp.where(term_r == 2, slope * hi_r, slope * lo_r)))
        tq = (dq[:, hh * 128:(hh + 1) * 128] * diff_scale).T
        q1t_ref[0, hh] = jnp.concatenate([tq[:DIFF_DIM], qaug, zpad], axis=0).astype(jnp.bfloat16)
        q2t_ref[0, hh] = jnp.concatenate([tq[DIFF_DIM:], qaug, zpad], axis=0).astype(jnp.bfloat16)
        tk = dk[:, hh * 128:(hh + 1) * 128]
        k1_ref[0, hh] = jnp.where(lane < DIFF_DIM, tk, kaug).astype(jnp.bfloat16)
        k2_ref[0, hh] = jnp.where(lane < DIFF_DIM, pltpu.roll(tk, DIFF_DIM, axis=1), kaug).astype(jnp.bfloat16)
        vtd_ref[0, hh, 0] = jnp.concatenate(
            [dv[:, hh * 128:(hh + 1) * 128].T, ones_rows], axis=0).astype(jnp.bfloat16)


def _inproj(x, win, qg, wq, kvg, wkv, rope):
    b, s, _ = x.shape
    nc = s // KV_CHUNK
    tm = ROW_TILE
    bf = jnp.bfloat16
    const = lambda shape: pl.BlockSpec(shape, lambda bb, i: (0,) * len(shape))
    out_shape = (
        jax.ShapeDtypeStruct((b, MLA_HEADS, MLA_QK, s), bf),
        jax.ShapeDtypeStruct((b, MLA_HEADS, s, MLA_QK), bf),
        jax.ShapeDtypeStruct((b, MLA_HEADS, nc, V_ROWS, KV_CHUNK), bf),
        jax.ShapeDtypeStruct((b, DIFF_HEADS, LANE, s), bf),
        jax.ShapeDtypeStruct((b, DIFF_HEADS, LANE, s), bf),
        jax.ShapeDtypeStruct((b, DIFF_HEADS, s, LANE), bf),
        jax.ShapeDtypeStruct((b, DIFF_HEADS, s, LANE), bf),
        jax.ShapeDtypeStruct((b, DIFF_HEADS, nc, V_ROWS, KV_CHUNK), bf),
    )
    out_specs = (
        pl.BlockSpec((1, MLA_HEADS, MLA_QK, tm), lambda bb, i: (bb, 0, 0, i)),
        pl.BlockSpec((1, MLA_HEADS, tm, MLA_QK), lambda bb, i: (bb, 0, i, 0)),
        pl.BlockSpec((1, MLA_HEADS, 1, V_ROWS, tm), lambda bb, i: (bb, 0, i, 0, 0)),
        pl.BlockSpec((1, DIFF_HEADS, LANE, tm), lambda bb, i: (bb, 0, 0, i)),
        pl.BlockSpec((1, DIFF_HEADS, LANE, tm), lambda bb, i: (bb, 0, 0, i)),
        pl.BlockSpec((1, DIFF_HEADS, tm, LANE), lambda bb, i: (bb, 0, i, 0)),
        pl.BlockSpec((1, DIFF_HEADS, tm, LANE), lambda bb, i: (bb, 0, i, 0)),
        pl.BlockSpec((1, DIFF_HEADS, 1, V_ROWS, tm), lambda bb, i: (bb, 0, i, 0, 0)),
    )
    return pl.pallas_call(
        _inproj_kernel,
        grid=(b, s // tm),
        in_specs=[
            pl.BlockSpec((1, tm, D_MODEL), lambda bb, i: (bb, i, 0)),
            const((D_MODEL, W_IN_COLS)),
            const((1, MLA_Q_RANK)),
            const((MLA_Q_RANK, MLA_HEADS * MLA_QK)),
            const((1, MLA_KV_RANK)),
            const((MLA_KV_RANK, MLA_HEADS * 256)),
            pl.BlockSpec((tm, LANE), lambda bb, i: (i, 0)),
        ],
        out_specs=out_specs,
        out_shape=out_shape,
        compiler_params=pltpu.CompilerParams(
            dimension_semantics=("parallel", "parallel"), vmem_limit_bytes=VMEM_LIMIT),
        name="inproj",
    )(x, win, qg, wq, kvg, wkv, rope)


def _qk_stage(k_tile, q_tile, s_ref, mb_ref, corr=None):
    s = _dot(k_tile, q_tile)
    if corr is not None:
        s = s - corr
    s_ref[...] = s
    mb_ref[...] = jnp.max(s, axis=0, keepdims=True)


def _sm_stage(s_ref, mb_ref, p_ref, m_ref):
    m_old = m_ref[...]
    m_new = jnp.maximum(m_old, mb_ref[...])
    p_ref[...] = jnp.exp2(s_ref[...] - m_new).astype(p_ref.dtype)
    m_ref[...] = m_new
    return jnp.exp2(m_old - m_new)


def _pv_stage(vt_tiles, p_ref, acc_ref, alpha):
    acc = acc_ref[...]
    for n, vt_tile in enumerate(vt_tiles):
        acc = acc + _dot(vt_tile, p_ref[n * KV_CHUNK:(n + 1) * KV_CHUNK, :])
    acc_ref[...] = acc if alpha is None else alpha * acc


def _key_rows(c, tk):
    return pl.ds(pl.multiple_of(c * tk, tk), tk)


def _mla_kernel(qt_ref, k_ref, vt_ref, o_ref, s0, s1, p0, p1, mb0, mb1, m_sc, acc_sc):
    tk = s0.shape[0]
    sub = tk // KV_CHUNK
    nk = vt_ref.shape[2] // sub
    s_bufs, p_bufs, mb_bufs = (s0, s1), (p0, p1), (mb0, mb1)
    m_sc[...] = jnp.full_like(m_sc, NEG_INF)
    acc_sc[...] = jnp.zeros_like(acc_sc)

    def qk(c, slot):
        _qk_stage(k_ref[0, 0, _key_rows(c, tk), :], qt_ref[0, 0], s_bufs[slot], mb_bufs[slot])

    def sm(slot):
        return _sm_stage(s_bufs[slot], mb_bufs[slot], p_bufs[slot], m_sc)

    def pv(c, slot, alpha):
        _pv_stage([vt_ref[0, 0, c * sub + n] for n in range(sub)], p_bufs[slot], acc_sc, alpha)

    def step(c, slot):
        qk(c + 1, 1 - slot)
        alpha = sm(slot)
        pv(c - 1, 1 - slot, alpha)

    qk(0, 0)
    qk(1, 1)
    sm(0)

    def pair(t, carry):
        step(2 * t + 1, 1)
        step(2 * t + 2, 0)
        return carry

    lax.fori_loop(0, nk // 2 - 1, pair, 0)
    alpha = sm(1)
    pv(nk - 2, 0, alpha)
    pv(nk - 1, 1, None)
    acc = acc_sc[...]
    o = acc[:MLA_V] / acc[MLA_V:MLA_V + 1]
    o_ref[0] = o.T.astype(o_ref.dtype)


def _mla_attention(qt, k, vt):
    b, hds, _, s = qt.shape
    nc = vt.shape[2]
    tq, tk = MLA_Q_TILE, MLA_K_TILE
    assert tk % KV_CHUNK == 0 and s % (2 * tk) == 0
    f32, bf = jnp.float32, jnp.bfloat16
    return pl.pallas_call(
        _mla_kernel,
        grid=(b, hds, s // tq),
        in_specs=[
            pl.BlockSpec((1, 1, MLA_QK, tq), lambda bb, h, i: (bb, h, 0, i)),
            pl.BlockSpec((1, 1, s, MLA_QK), lambda bb, h, i: (bb, h, 0, 0)),
            pl.BlockSpec((1, 1, nc, V_ROWS, KV_CHUNK), lambda bb, h, i: (bb, h, 0, 0, 0)),
        ],
        out_specs=pl.BlockSpec((1, tq, MLA_V), lambda bb, h, i: (bb, i, h)),
        out_shape=jax.ShapeDtypeStruct((b, s, hds * MLA_V), bf),
        scratch_shapes=[
            pltpu.VMEM((tk, tq), f32), pltpu.VMEM((tk, tq), f32),
            pltpu.VMEM((tk, tq), bf), pltpu.VMEM((tk, tq), bf),
            pltpu.VMEM((1, tq), f32), pltpu.VMEM((1, tq), f32),
            pltpu.VMEM((1, tq), f32),
            pltpu.VMEM((V_ROWS, tq), f32),
        ],
        compiler_params=pltpu.CompilerParams(
            dimension_semantics=("parallel", "parallel", "arbitrary"), vmem_limit_bytes=VMEM_LIMIT),
        name="mla_attn",
    )(qt, k, vt)


def _diff_kernel(q1t_ref, q2t_ref, k1_ref, k2_ref, vt_ref, lam_ref, linit_ref, g_ref, o_ref,
                 qv_sc, corr_sc, s_sc, p_sc, mb_sc, m_sc, acc_sc):
    h = pl.program_id(1)
    qi = pl.program_id(2)
    tq = q1t_ref.shape[3]
    sub = tq // KV_CHUNK
    nk = vt_ref.shape[2] // sub
    k_refs = (k1_ref, k2_ref)
    m_sc[...] = jnp.full_like(m_sc, NEG_INF)
    acc_sc[...] = jnp.zeros_like(acc_sc)
    row = lax.broadcasted_iota(jnp.int32, (LANE, tq), 0)
    for mi, qref in enumerate((q1t_ref, q2t_ref)):
        qq = qref[0, 0]
        qv_sc[mi] = qq
        qv_sc[2 + mi] = jnp.where(row < DIFF_DIM, qq, -qq)

    @pl.when(qi == 0)
    def _():
        jj = lax.broadcasted_iota(jnp.int32, (tq, tq), 0)
        ii = lax.broadcasted_iota(jnp.int32, (tq, tq), 1)
        expo = jnp.full((1, tq), 127 + 1, jnp.int32) - 2 * (h + 1)
        slope2 = lax.bitcast_convert_type(expo << 23, jnp.float32)
        corr_sc[...] = (slope2 * LOG2E) * jnp.maximum(jj - ii, 0).astype(jnp.float32)

    def chunk(j):
        return lax.rem(qi + j, nk)

    def qk(j, slot, diagonal=False):
        c = chunk(j)
        variant = 0 if diagonal else jnp.where(c > qi, 2, 0)
        for mi in range(2):
            _qk_stage(k_refs[mi][0, 0, _key_rows(c, tq), :], qv_sc[variant + mi],
                      s_sc.at[2 * slot + mi], mb_sc.at[2 * slot + mi],
                      corr_sc[...] if diagonal else None)

    def sm(slot):
        return [_sm_stage(s_sc.at[2 * slot + mi], mb_sc.at[2 * slot + mi], p_sc.at[2 * slot + mi],
                          m_sc.at[mi]) for mi in range(2)]

    def pv(j, slot, alphas):
        c = chunk(j)
        vt = [vt_ref[0, 0, c * sub + n] for n in range(sub)]
        for mi in range(2):
            _pv_stage(vt, p_sc.at[2 * slot + mi], acc_sc.at[mi], alphas[mi])

    def step(j, slot):
        qk(j + 1, 1 - slot)
        alphas = sm(slot)
        pv(j - 1, 1 - slot, alphas)

    qk(0, 0, diagonal=True)
    qk(1, 1)
    sm(0)

    def pair(t, carry):
        step(2 * t + 1, 1)
        step(2 * t + 2, 0)
        return carry

    lax.fori_loop(0, nk // 2 - 1, pair, 0)
    alphas = sm(1)
    pv(nk - 2, 0, alphas)
    pv(nk - 1, 1, (None, None))

    lamp = lam_ref[...]
    linit = linit_ref[...]
    lam = (jnp.exp(jnp.sum(lamp[0:1] * lamp[1:2], axis=-1, keepdims=True))
           - jnp.exp(jnp.sum(lamp[2:3] * lamp[3:4], axis=-1, keepdims=True)) + linit)
    nv = 2 * DIFF_DIM
    a1, a2 = acc_sc[0], acc_sc[1]
    o = a1[:nv] / a1[nv:nv + 1] - lam * (a2[:nv] / a2[nv:nv + 1])
    o = _rms_norm(o.T, g_ref[...]) * (1.0 - linit)
    o_ref[0] = o.astype(o_ref.dtype)


def _diff_attention(q1t, q2t, k1, k2, vt, lamp, linit, subln_g):
    b, hds, _, s = q1t.shape
    nc = vt.shape[2]
    t = DIFF_TILE
    assert t % KV_CHUNK == 0 and s % (2 * t) == 0
    f32, bf = jnp.float32, jnp.bfloat16
    qspec = pl.BlockSpec((1, 1, LANE, t), lambda bb, h, i: (bb, h, 0, i))
    kspec = pl.BlockSpec((1, 1, s, LANE), lambda bb, h, i: (bb, h, 0, 0))
    const = lambda shape: pl.BlockSpec(shape, lambda bb, h, i: (0,) * len(shape))
    return pl.pallas_call(
        _diff_kernel,
        grid=(b, hds, s // t),
        in_specs=[
            qspec, qspec, kspec, kspec,
            pl.BlockSpec((1, 1, nc, V_ROWS, KV_CHUNK), lambda bb, h, i: (bb, h, 0, 0, 0)),
            const((4, DIFF_DIM)), const((1, 1)), const((1, 2 * DIFF_DIM)),
        ],
        out_specs=pl.BlockSpec((1, t, 2 * DIFF_DIM), lambda bb, h, i: (bb, i, h)),
        out_shape=jax.ShapeDtypeStruct((b, s, hds * 2 * DIFF_DIM), bf),
        scratch_shapes=[
            pltpu.VMEM((4, LANE, t), bf),
            pltpu.VMEM((t, t), f32),
            pltpu.VMEM((4, t, t), f32),
            pltpu.VMEM((4, t, t), bf),
            pltpu.VMEM((4, 1, t), f32),
            pltpu.VMEM((2, 1, t), f32),
            pltpu.VMEM((2, V_ROWS, t), f32),
        ],
        compiler_params=pltpu.CompilerParams(
            dimension_semantics=("parallel", "parallel", "arbitrary"), vmem_limit_bytes=VMEM_LIMIT),
        name="diff_attn",
    )(q1t, q2t, k1, k2, vt, lamp, linit, subln_g)


def _outproj_ln_kernel(x_ref, om_ref, od_ref, w_ref, g_ref, b_ref, o_ref):
    nm = om_ref.shape[1]
    mix = _dot(om_ref[...], w_ref[:nm, :]) + _dot(od_ref[...], w_ref[nm:, :])
    o_ref[...] = _layer_norm(ALPHA * x_ref[...] + mix, g_ref[...], b_ref[...])


def _outproj_ln(x, om, od, w, g, b):
    t = x.shape[0]
    nm, nd = om.shape[1], od.shape[1]
    return pl.pallas_call(
        _outproj_ln_kernel,
        grid=(t // ROW_TILE,),
        in_specs=[
            pl.BlockSpec((ROW_TILE, D_MODEL), lambda i: (i, 0)),
            pl.BlockSpec((ROW_TILE, nm), lambda i: (i, 0)),
            pl.BlockSpec((ROW_TILE, nd), lambda i: (i, 0)),
            pl.BlockSpec((nm + nd, D_MODEL), lambda i: (0, 0)),
            pl.BlockSpec((1, D_MODEL), lambda i: (0, 0)),
            pl.BlockSpec((1, D_MODEL), lambda i: (0, 0)),
        ],
        out_specs=pl.BlockSpec((ROW_TILE, D_MODEL), lambda i: (i, 0)),
        out_shape=jax.ShapeDtypeStruct((t, D_MODEL), jnp.float32),
        compiler_params=pltpu.CompilerParams(
            dimension_semantics=("parallel",), vmem_limit_bytes=VMEM_LIMIT),
        name="outproj_ln",
    )(x, om, od, w, g, b)


def _swap_halves(w):
    half = w.shape[-1] // 2
    return jnp.concatenate([w[..., half:], w[..., :half]], axis=-1)


def _prep_w_in(w_in):
    c_q = w_in[..., :256]
    c_kv = w_in[..., 256:384]
    k_rope = w_in[..., 384:448]
    rest = w_in[..., 448:]
    return jnp.concatenate([c_q, c_kv, k_rope, _swap_halves(k_rope), rest], axis=-1).astype(jnp.bfloat16)


def _prep_w_q_up(w_q_up):
    l, r, _ = w_q_up.shape
    w = w_q_up.reshape(l, r, MLA_HEADS, MLA_NOPE + MLA_ROPE)
    rope = w[..., MLA_NOPE:]
    w = jnp.concatenate([w[..., :MLA_NOPE], rope, _swap_halves(rope)], axis=-1)
    return w.reshape(l, r, MLA_HEADS * MLA_QK).astype(jnp.bfloat16)


def _rope_table(s):
    inv_freq = ROPE_THETA ** (-jnp.arange(0, MLA_ROPE, 2, dtype=jnp.float32) / MLA_ROPE)
    ang = jnp.arange(s, dtype=jnp.int32).astype(jnp.float32)[:, None] * inv_freq[None, :]
    cos, sin = jnp.cos(ang), jnp.sin(ang)
    return jnp.concatenate([cos, cos, -sin, sin], axis=-1)


def kernel(x, ffn1_w_gate, ffn1_w_up, ffn1_w_down, ln1_g, ln1_b, w_in, q_norm_g, w_q_up, kv_norm_g, w_kv_up, diff_lambda_q1, diff_lambda_k1, diff_lambda_q2, diff_lambda_k2, diff_subln_g, w_out, ln2_g, ln2_b, ffn2_w_gate, ffn2_w_up, ffn2_w_down, ln3_g, ln3_b):
    b, s, d = x.shape
    bf = jnp.bfloat16
    rope = _rope_table(s)
    w_in_p = _prep_w_in(w_in)
    w_q_p = _prep_w_q_up(w_q_up)
    w_kv_p = w_kv_up.astype(bf)
    w_out_p = w_out.astype(bf)
    f1 = (ffn1_w_gate.astype(bf), ffn1_w_up.astype(bf), ffn1_w_down.astype(bf))
    f2 = (ffn2_w_gate.astype(bf), ffn2_w_up.astype(bf), ffn2_w_down.astype(bf))
    lamp = jnp.stack([diff_lambda_q1, diff_lambda_k1, diff_lambda_q2, diff_lambda_k2], axis=1)
    row = lambda v, l: v[l][None, :]

    xf = x.reshape(b * s, d)
    for l in range(DEPTH):
        linit = jnp.full((1, 1), 0.8 - 0.6 * math.exp(-0.3 * l), jnp.float32)
        xf = _ffn_ln(xf, f1[0][l], f1[1][l], f1[2][l], row(ln1_g, l), row(ln1_b, l))
        qt, km, vtm, q1t, q2t, k1, k2, vtd = _inproj(
            xf.reshape(b, s, d), w_in_p[l], row(q_norm_g, l), w_q_p[l], row(kv_norm_g, l), w_kv_p[l], rope)
        o_mla = _mla_attention(qt, km, vtm)
        o_diff = _diff_attention(q1t, q2t, k1, k2, vtd, lamp[l], linit, row(diff_subln_g, l))
        xf = _outproj_ln(xf, o_mla.reshape(b * s, -1), o_diff.reshape(b * s, -1), w_out_p[l],
                         row(ln2_g, l), row(ln2_b, l))
        xf = _ffn_ln(xf, f2[0][l], f2[1][l], f2[2][l], row(ln3_g, l), row(ln3_b, l))
    return xf.reshape(b, s, d)
```

```python
import math

import jax
import jax.numpy as jnp
import numpy as np
from jax import lax
from jax.experimental import pallas as pl
from jax.experimental.pallas import tpu as pltpu

D_MODEL = 1024
DEPTH = 4
MLA_HEADS = 4
MLA_NOPE = 128
MLA_ROPE = 64
MLA_V = 128
MLA_Q_RANK = 256
MLA_KV_RANK = 128
DIFF_HEADS = 4
DIFF_DIM = 64
D_FF = 2816
ROPE_THETA = 10000.0
NORM_EPS = 1e-5
ALPHA = (2 * DEPTH) ** 0.25

LANE = 128
ROW_TILE = 512
KV_CHUNK = ROW_TILE
DIFF_TILE = 1024
MLA_Q_TILE = 1024
MLA_K_TILE = 1024
POS_SPLIT = 128
VMEM_LIMIT = 58 * 1024 * 1024

MLA_QK = MLA_NOPE + 2 * MLA_ROPE
W_IN_COLS = 2048
NEG_INF = float("-inf")
LOG2E = math.log2(math.e)
AUG_ROWS = 16
V_ROWS = MLA_V + AUG_ROWS


def _bf16_pieces(x, n):
    out = []
    for _ in range(n):
        piece = float(np.asarray(x, dtype=jnp.bfloat16).astype(np.float32))
        out.append(piece)
        x -= piece
    return tuple(out)


LOG2E_PIECES = _bf16_pieces(LOG2E, 3)


def _dot(a, b):
    return jnp.dot(a, b, preferred_element_type=jnp.float32)


def _layer_norm(y, g, b):
    mu = jnp.mean(y, axis=-1, keepdims=True)
    d = y - mu
    var = jnp.mean(d * d, axis=-1, keepdims=True)
    return d * lax.rsqrt(var + NORM_EPS) * g + b


def _rms_norm(y, g):
    return y * lax.rsqrt(jnp.mean(y * y, axis=-1, keepdims=True) + NORM_EPS) * g


def _ffn_ln_kernel(x_ref, wg_ref, wu_ref, wd_ref, g_ref, b_ref, o_ref):
    x = x_ref[...]
    xb = x.astype(jnp.bfloat16)
    gate = _dot(xb, wg_ref[...])
    up = _dot(xb, wu_ref[...])
    h = gate / (1.0 + jnp.exp(-gate)) * up
    ffn = _dot(h.astype(jnp.bfloat16), wd_ref[...])
    o_ref[...] = _layer_norm(ALPHA * x + 0.5 * ffn, g_ref[...], b_ref[...])


def _ffn_ln(x, wg, wu, wd, g, b):
    t = x.shape[0]
    resident = lambda shape: pl.BlockSpec(shape, lambda i: (0, 0), pipeline_mode=pl.Buffered(1))
    return pl.pallas_call(
        _ffn_ln_kernel,
        grid=(t // ROW_TILE,),
        in_specs=[
            pl.BlockSpec((ROW_TILE, D_MODEL), lambda i: (i, 0)),
            resident((D_MODEL, D_FF)), resident((D_MODEL, D_FF)), resident((D_FF, D_MODEL)),
            resident((1, D_MODEL)), resident((1, D_MODEL)),
        ],
        out_specs=pl.BlockSpec((ROW_TILE, D_MODEL), lambda i: (i, 0)),
        out_shape=jax.ShapeDtypeStruct((t, D_MODEL), jnp.float32),
        compiler_params=pltpu.CompilerParams(
            dimension_semantics=("parallel",), vmem_limit_bytes=VMEM_LIMIT),
        name="ffn_ln",
    )(x, wg, wu, wd, g, b)


def _alibi_slope(h):
    return 2.0 ** (-8.0 * (h + 1) / DIFF_HEADS)


def _inproj_kernel(x_ref, win_ref, qg_ref, wq_ref, kvg_ref, wkv_ref, rope_ref,
                   qt_ref, km_ref, vtm_ref, q1t_ref, q2t_ref, k1_ref, k2_ref, vtd_ref):
    i = pl.program_id(1)
    tm = x_ref.shape[1]
    xb = x_ref[0].astype(jnp.bfloat16)
    h = _dot(xb, win_ref[...])
    c_q = h[:, 0:256]
    c_kv = h[:, 256:384]
    kr = h[:, 384:512]
    dq = h[:, 512:1024]
    dk = h[:, 1024:1536]
    dv = h[:, 1536:2048]
    rope = rope_ref[...]

    q = _dot(_rms_norm(c_q, qg_ref[...]).astype(jnp.bfloat16), wq_ref[...])
    kv = _dot(_rms_norm(c_kv, kvg_ref[...]).astype(jnp.bfloat16), wkv_ref[...])
    krt = kr * rope
    k_rope = krt + pltpu.roll(krt, MLA_ROPE, axis=1)
    mla_scale = (MLA_NOPE + MLA_ROPE) ** -0.5 * LOG2E
    ones_rows = jnp.where(lax.broadcasted_iota(jnp.int32, (AUG_ROWS, tm), 0) == 0, 1.0, 0.0)
    for hh in range(MLA_HEADS):
        qh = q[:, hh * MLA_QK:(hh + 1) * MLA_QK]
        qh = jnp.concatenate([qh[:, :MLA_NOPE], qh[:, MLA_NOPE:] * rope], axis=1) * mla_scale
        qt_ref[0, hh] = qh.T.astype(jnp.bfloat16)
        kvh = kv[:, hh * 256:(hh + 1) * 256]
        km_ref[0, hh] = jnp.concatenate([kvh[:, :MLA_NOPE], k_rope], axis=1).astype(jnp.bfloat16)
        vtm_ref[0, hh, 0] = jnp.concatenate([kvh[:, MLA_NOPE:].T, ones_rows], axis=0).astype(jnp.bfloat16)

    pos_c = i * tm + lax.broadcasted_iota(jnp.int32, (AUG_ROWS, tm), 1)
    row = lax.broadcasted_iota(jnp.int32, (AUG_ROWS, tm), 0)
    hi_c = (pos_c & -POS_SPLIT).astype(jnp.float32)
    lo_c = (pos_c & (POS_SPLIT - 1)).astype(jnp.float32)
    pos_r = i * tm + lax.broadcasted_iota(jnp.int32, (tm, LANE), 0)
    lane = lax.broadcasted_iota(jnp.int32, (tm, LANE), 1)
    hi_r = (pos_r & -POS_SPLIT).astype(jnp.float32)
    lo_r = (pos_r & (POS_SPLIT - 1)).astype(jnp.float32)
    diff_scale = DIFF_DIM ** -0.5 * LOG2E
    zpad = jnp.zeros((DIFF_DIM - AUG_ROWS, tm), jnp.float32)
    def piece(slot):
        return jnp.where(slot < 4, LOG2E_PIECES[0], jnp.where(slot < 8, LOG2E_PIECES[1], LOG2E_PIECES[2]))

    slot_r = lane - DIFF_DIM
    piece_c, term_c = piece(row), row & 3
    piece_r, term_r = piece(slot_r), slot_r & 3
    for hh in range(DIFF_HEADS):
        slope = _alibi_slope(hh)
        qaug = jnp.where(row >= 12, 0.0,
                         jnp.where(term_c == 0, -slope * hi_c,
                                   jnp.where(term_c == 1, -slope * lo_c, piece_c)))
        kaug = jnp.where(slot_r >= 12, 0.0,
                         jnp.where(term_r < 2, piece_r,
                                   jnp.where(term_r == 2, slope * hi_r, slope * lo_r)))
        tq = (dq[:, hh * 128:(hh + 1) * 128] * diff_scale).T
        q1t_ref[0, hh] = jnp.concatenate([tq[:DIFF_DIM], qaug, zpad], axis=0).astype(jnp.bfloat16)
        q2t_ref[0, hh] = jnp.concatenate([tq[DIFF_DIM:], qaug, zpad], axis=0).astype(jnp.bfloat16)
        tk = dk[:, hh * 128:(hh + 1) * 128]
        k1_ref[0, hh] = jnp.where(lane < DIFF_DIM, tk, kaug).astype(jnp.bfloat16)
        k2_ref[0, hh] = jnp.where(lane < DIFF_DIM, pltpu.roll(tk, DIFF_DIM, axis=1), kaug).astype(jnp.bfloat16)
        vtd_ref[0, hh, 0] = jnp.concatenate(
            [dv[:, hh * 128:(hh + 1) * 128].T, ones_rows], axis=0).astype(jnp.bfloat16)


def _inproj(x, win, qg, wq, kvg, wkv, rope):
    b, s, _ = x.shape
    nc = s // KV_CHUNK
    tm = ROW_TILE
    bf = jnp.bfloat16
    const = lambda shape: pl.BlockSpec(shape, lambda bb, i: (0,) * len(shape))
    out_shape = (
        jax.ShapeDtypeStruct((b, MLA_HEADS, MLA_QK, s), bf),
        jax.ShapeDtypeStruct((b, MLA_HEADS, s, MLA_QK), bf),
        jax.ShapeDtypeStruct((b, MLA_HEADS, nc, V_ROWS, KV_CHUNK), bf),
        jax.ShapeDtypeStruct((b, DIFF_HEADS, LANE, s), bf),
        jax.ShapeDtypeStruct((b, DIFF_HEADS, LANE, s), bf),
        jax.ShapeDtypeStruct((b, DIFF_HEADS, s, LANE), bf),
        jax.ShapeDtypeStruct((b, DIFF_HEADS, s, LANE), bf),
        jax.ShapeDtypeStruct((b, DIFF_HEADS, nc, V_ROWS, KV_CHUNK), bf),
    )
    out_specs = (
        pl.BlockSpec((1, MLA_HEADS, MLA_QK, tm), lambda bb, i: (bb, 0, 0, i)),
        pl.BlockSpec((1, MLA_HEADS, tm, MLA_QK), lambda bb, i: (bb, 0, i, 0)),
        pl.BlockSpec((1, MLA_HEADS, 1, V_ROWS, tm), lambda bb, i: (bb, 0, i, 0, 0)),
        pl.BlockSpec((1, DIFF_HEADS, LANE, tm), lambda bb, i: (bb, 0, 0, i)),
        pl.BlockSpec((1, DIFF_HEADS, LANE, tm), lambda bb, i: (bb, 0, 0, i)),
        pl.BlockSpec((1, DIFF_HEADS, tm, LANE), lambda bb, i: (bb, 0, i, 0)),
        pl.BlockSpec((1, DIFF_HEADS, tm, LANE), lambda bb, i: (bb, 0, i, 0)),
        pl.BlockSpec((1, DIFF_HEADS, 1, V_ROWS, tm), lambda bb, i: (bb, 0, i, 0, 0)),
    )
    return pl.pallas_call(
        _inproj_kernel,
        grid=(b, s // tm),
        in_specs=[
            pl.BlockSpec((1, tm, D_MODEL), lambda bb, i: (bb, i, 0)),
            const((D_MODEL, W_IN_COLS)),
            const((1, MLA_Q_RANK)),
            const((MLA_Q_RANK, MLA_HEADS * MLA_QK)),
            const((1, MLA_KV_RANK)),
            const((MLA_KV_RANK, MLA_HEADS * 256)),
            pl.BlockSpec((tm, LANE), lambda bb, i: (i, 0)),
        ],
        out_specs=out_specs,
        out_shape=out_shape,
        compiler_params=pltpu.CompilerParams(
            dimension_semantics=("parallel", "parallel"), vmem_limit_bytes=VMEM_LIMIT),
        name="inproj",
    )(x, win, qg, wq, kvg, wkv, rope)


def _qk_stage(k_tile, q_tile, s_ref, mb_ref, corr=None):
    s = _dot(k_tile, q_tile)
    if corr is not None:
        s = s - corr
    s_ref[...] = s
    mb_ref[...] = jnp.max(s, axis=0, keepdims=True)


def _sm_stage(s_ref, mb_ref, p_ref, m_ref):
    m_old = m_ref[...]
    m_new = jnp.maximum(m_old, mb_ref[...])
    p_ref[...] = jnp.exp2(s_ref[...] - m_new).astype(p_ref.dtype)
    m_ref[...] = m_new
    return jnp.exp2(m_old - m_new)


def _pv_stage(vt_tiles, p_ref, acc_ref, alpha):
    acc = acc_ref[...]
    for n, vt_tile in enumerate(vt_tiles):
        acc = acc + _dot(vt_tile, p_ref[n * KV_CHUNK:(n + 1) * KV_CHUNK, :])
    acc_ref[...] = acc if alpha is None else alpha * acc


def _key_rows(c, tk):
    return pl.ds(pl.multiple_of(c * tk, tk), tk)


def _mla_kernel(qt_ref, k_ref, vt_ref, o_ref, s0, s1, p0, p1, mb0, mb1, m_sc, acc_sc):
    tk = s0.shape[0]
    sub = tk // KV_CHUNK
    nk = vt_ref.shape[2] // sub
    s_bufs, p_bufs, mb_bufs = (s0, s1), (p0, p1), (mb0, mb1)
    m_sc[...] = jnp.full_like(m_sc, NEG_INF)
    acc_sc[...] = jnp.zeros_like(acc_sc)

    def qk(c, slot):
        _qk_stage(k_ref[0, 0, _key_rows(c, tk), :], qt_ref[0, 0], s_bufs[slot], mb_bufs[slot])

    def sm(slot):
        return _sm_stage(s_bufs[slot], mb_bufs[slot], p_bufs[slot], m_sc)

    def pv(c, slot, alpha):
        _pv_stage([vt_ref[0, 0, c * sub + n] for n in range(sub)], p_bufs[slot], acc_sc, alpha)

    def step(c, slot):
        qk(c + 1, 1 - slot)
        alpha = sm(slot)
        pv(c - 1, 1 - slot, alpha)

    qk(0, 0)
    qk(1, 1)
    sm(0)

    def pair(t, carry):
        step(2 * t + 1, 1)
        step(2 * t + 2, 0)
        return carry

    lax.fori_loop(0, nk // 2 - 1, pair, 0)
    alpha = sm(1)
    pv(nk - 2, 0, alpha)
    pv(nk - 1, 1, None)
    acc = acc_sc[...]
    o = acc[:MLA_V] / acc[MLA_V:MLA_V + 1]
    o_ref[0] = o.T.astype(o_ref.dtype)


def _mla_attention(qt, k, vt):
    b, hds, _, s = qt.shape
    nc = vt.shape[2]
    tq, tk = MLA_Q_TILE, MLA_K_TILE
    assert tk % KV_CHUNK == 0 and s % (2 * tk) == 0
    f32, bf = jnp.float32, jnp.bfloat16
    return pl.pallas_call(
        _mla_kernel,
        grid=(b, hds, s // tq),
        in_specs=[
            pl.BlockSpec((1, 1, MLA_QK, tq), lambda bb, h, i: (bb, h, 0, i)),
            pl.BlockSpec((1, 1, s, MLA_QK), lambda bb, h, i: (bb, h, 0, 0)),
            pl.BlockSpec((1, 1, nc, V_ROWS, KV_CHUNK), lambda bb, h, i: (bb, h, 0, 0, 0)),
        ],
        out_specs=pl.BlockSpec((1, tq, MLA_V), lambda bb, h, i: (bb, i, h)),
        out_shape=jax.ShapeDtypeStruct((b, s, hds * MLA_V), bf),
        scratch_shapes=[
            pltpu.VMEM((tk, tq), f32), pltpu.VMEM((tk, tq), f32),
            pltpu.VMEM((tk, tq), bf), pltpu.VMEM((tk, tq), bf),
            pltpu.VMEM((1, tq), f32), pltpu.VMEM((1, tq), f32),
            pltpu.VMEM((1, tq), f32),
            pltpu.VMEM((V_ROWS, tq), f32),
        ],
        compiler_params=pltpu.CompilerParams(
            dimension_semantics=("parallel", "parallel", "arbitrary"), vmem_limit_bytes=VMEM_LIMIT),
        name="mla_attn",
    )(qt, k, vt)


def _diff_kernel(q1t_ref, q2t_ref, k1_ref, k2_ref, vt_ref, lam_ref, linit_ref, g_ref, o_ref,
                 qv_sc, corr_sc, s_sc, p_sc, mb_sc, m_sc, acc_sc):
    h = pl.program_id(1)
    qi = pl.program_id(2)
    tq = q1t_ref.shape[3]
    sub = tq // KV_CHUNK
    nk = vt_ref.shape[2] // sub
    k_refs = (k1_ref, k2_ref)
    m_sc[...] = jnp.full_like(m_sc, NEG_INF)
    acc_sc[...] = jnp.zeros_like(acc_sc)
    row = lax.broadcasted_iota(jnp.int32, (LANE, tq), 0)
    for mi, qref in enumerate((q1t_ref, q2t_ref)):
        qq = qref[0, 0]
        qv_sc[mi] = qq
        qv_sc[2 + mi] = jnp.where(row < DIFF_DIM, qq, -qq)

    @pl.when(qi == 0)
    def _():
        jj = lax.broadcasted_iota(jnp.int32, (tq, tq), 0)
        ii = lax.broadcasted_iota(jnp.int32, (tq, tq), 1)
        expo = jnp.full((1, tq), 127 + 1, jnp.int32) - 2 * (h + 1)
        slope2 = lax.bitcast_convert_type(expo << 23, jnp.float32)
        corr_sc[...] = (slope2 * LOG2E) * jnp.maximum(jj - ii, 0).astype(jnp.float32)

    def chunk(j):
        return lax.rem(qi + j, nk)

    def qk(j, slot, diagonal=False):
        c = chunk(j)
        variant = 0 if diagonal else jnp.where(c > qi, 2, 0)
        for mi in range(2):
            _qk_stage(k_refs[mi][0, 0, _key_rows(c, tq), :], qv_sc[variant + mi],
                      s_sc.at[2 * slot + mi], mb_sc.at[2 * slot + mi],
                      corr_sc[...] if diagonal else None)

    def sm(slot):
        return [_sm_stage(s_sc.at[2 * slot + mi], mb_sc.at[2 * slot + mi], p_sc.at[2 * slot + mi],
                          m_sc.at[mi]) for mi in range(2)]

    def pv(j, slot, alphas):
        c = chunk(j)
        vt = [vt_ref[0, 0, c * sub + n] for n in range(sub)]
        for mi in range(2):
            _pv_stage(vt, p_sc.at[2 * slot + mi], acc_sc.at[mi], alphas[mi])

    def step(j, slot):
        qk(j + 1, 1 - slot)
        alphas = sm(slot)
        pv(j - 1, 1 - slot, alphas)

    qk(0, 0, diagonal=True)
    qk(1, 1)
    sm(0)

    def pair(t, carry):
        step(2 * t + 1, 1)
        step(2 * t + 2, 0)
        return carry

    lax.fori_loop(0, nk // 2 - 1, pair, 0)
    alphas = sm(1)
    pv(nk - 2, 0, alphas)
    pv(nk - 1, 1, (None, None))

    lamp = lam_ref[...]
    linit = linit_ref[...]
    lam = (jnp.exp(jnp.sum(lamp[0:1] * lamp[1:2], axis=-1, keepdims=True))
           - jnp.exp(jnp.sum(lamp[2:3] * lamp[3:4], axis=-1, keepdims=True)) + linit)
    nv = 2 * DIFF_DIM
    a1, a2 = acc_sc[0], acc_sc[1]
    o = a1[:nv] / a1[nv:nv + 1] - lam * (a2[:nv] / a2[nv:nv + 1])
    o = _rms_norm(o.T, g_ref[...]) * (1.0 - linit)
    o_ref[0] = o.astype(o_ref.dtype)


def _diff_attention(q1t, q2t, k1, k2, vt, lamp, linit, subln_g):
    b, hds, _, s = q1t.shape
    nc = vt.shape[2]
    t = DIFF_TILE
    assert t % KV_CHUNK == 0 and s % (2 * t) == 0
    f32, bf = jnp.float32, jnp.bfloat16
    qspec = pl.BlockSpec((1, 1, LANE, t), lambda bb, h, i: (bb, h, 0, i))
    kspec = pl.BlockSpec((1, 1, s, LANE), lambda bb, h, i: (bb, h, 0, 0))
    const = lambda shape: pl.BlockSpec(shape, lambda bb, h, i: (0,) * len(shape))
    return pl.pallas_call(
        _diff_kernel,
        grid=(b, hds, s // t),
        in_specs=[
            qspec, qspec, kspec, kspec,
            pl.BlockSpec((1, 1, nc, V_ROWS, KV_CHUNK), lambda bb, h, i: (bb, h, 0, 0, 0)),
            const((4, DIFF_DIM)), const((1, 1)), const((1, 2 * DIFF_DIM)),
        ],
        out_specs=pl.BlockSpec((1, t, 2 * DIFF_DIM), lambda bb, h, i: (bb, i, h)),
        out_shape=jax.ShapeDtypeStruct((b, s, hds * 2 * DIFF_DIM), bf),
        scratch_shapes=[
            pltpu.VMEM((4, LANE, t), bf),
            pltpu.VMEM((t, t), f32),
            pltpu.VMEM((4, t, t), f32),
            pltpu.VMEM((4, t, t), bf),
            pltpu.VMEM((4, 1, t), f32),
            pltpu.VMEM((2, 1, t), f32),
            pltpu.VMEM((2, V_ROWS, t), f32),
        ],
        compiler_params=pltpu.CompilerParams(
            dimension_semantics=("parallel", "parallel", "arbitrary"), vmem_limit_bytes=VMEM_LIMIT),
        name="diff_attn",
    )(q1t, q2t, k1, k2, vt, lamp, linit, subln_g)


def _outproj_ln_kernel(x_ref, om_ref, od_ref, w_ref, g_ref, b_ref, o_ref):
    nm = om_ref.shape[1]
    mix = _dot(om_ref[...], w_ref[:nm, :]) + _dot(od_ref[...], w_ref[nm:, :])
    o_ref[...] = _layer_norm(ALPHA * x_ref[...] + mix, g_ref[...], b_ref[...])


def _outproj_ln(x, om, od, w, g, b):
    t = x.shape[0]
    nm, nd = om.shape[1], od.shape[1]
    return pl.pallas_call(
        _outproj_ln_kernel,
        grid=(t // ROW_TILE,),
        in_specs=[
            pl.BlockSpec((ROW_TILE, D_MODEL), lambda i: (i, 0)),
            pl.BlockSpec((ROW_TILE, nm), lambda i: (i, 0)),
            pl.BlockSpec((ROW_TILE, nd), lambda i: (i, 0)),
            pl.BlockSpec((nm + nd, D_MODEL), lambda i: (0, 0)),
            pl.BlockSpec((1, D_MODEL), lambda i: (0, 0)),
            pl.BlockSpec((1, D_MODEL), lambda i: (0, 0)),
        ],
        out_specs=pl.BlockSpec((ROW_TILE, D_MODEL), lambda i: (i, 0)),
        out_shape=jax.ShapeDtypeStruct((t, D_MODEL), jnp.float32),
        compiler_params=pltpu.CompilerParams(
            dimension_semantics=("parallel",), vmem_limit_bytes=VMEM_LIMIT),
        name="outproj_ln",
    )(x, om, od, w, g, b)


def _swap_halves(w):
    half = w.shape[-1] // 2
    return jnp.concatenate([w[..., half:], w[..., :half]], axis=-1)


def _prep_w_in(w_in):
    c_q = w_in[..., :256]
    c_kv = w_in[..., 256:384]
    k_rope = w_in[..., 384:448]
    rest = w_in[..., 448:]
    return jnp.concatenate([c_q, c_kv, k_rope, _swap_halves(k_rope), rest], axis=-1).astype(jnp.bfloat16)


def _prep_w_q_up(w_q_up):
    l, r, _ = w_q_up.shape
    w = w_q_up.reshape(l, r, MLA_HEADS, MLA_NOPE + MLA_ROPE)
    rope = w[..., MLA_NOPE:]
    w = jnp.concatenate([w[..., :MLA_NOPE], rope, _swap_halves(rope)], axis=-1)
    return w.reshape(l, r, MLA_HEADS * MLA_QK).astype(jnp.bfloat16)


def _rope_table(s):
    inv_freq = ROPE_THETA ** (-jnp.arange(0, MLA_ROPE, 2, dtype=jnp.float32) / MLA_ROPE)
    ang = jnp.arange(s, dtype=jnp.int32).astype(jnp.float32)[:, None] * inv_freq[None, :]
    cos, sin = jnp.cos(ang), jnp.sin(ang)
    return jnp.concatenate([cos, cos, -sin, sin], axis=-1)


def kernel(x, ffn1_w_gate, ffn1_w_up, ffn1_w_down, ln1_g, ln1_b, w_in, q_norm_g, w_q_up, kv_norm_g, w_kv_up, diff_lambda_q1, diff_lambda_k1, diff_lambda_q2, diff_lambda_k2, diff_subln_g, w_out, ln2_g, ln2_b, ffn2_w_gate, ffn2_w_up, ffn2_w_down, ln3_g, ln3_b):
    b, s, d = x.shape
    bf = jnp.bfloat16
    rope = _rope_table(s)
    w_in_p = _prep_w_in(w_in)
    w_q_p = _prep_w_q_up(w_q_up)
    w_kv_p = w_kv_up.astype(bf)
    w_out_p = w_out.astype(bf)
    f1 = (ffn1_w_gate.astype(bf), ffn1_w_up.astype(bf), ffn1_w_down.astype(bf))
    f2 = (ffn2_w_gate.astype(bf), ffn2_w_up.astype(bf), ffn2_w_down.astype(bf))
    lamp = jnp.stack([diff_lambda_q1, diff_lambda_k1, diff_lambda_q2, diff_lambda_k2], axis=1)
    row = lambda v, l: v[l][None, :]

    xf = x.reshape(b * s, d)
    for l in range(DEPTH):
        linit = jnp.full((1, 1), 0.8 - 0.6 * math.exp(-0.3 * l), jnp.float32)
        xf = _ffn_ln(xf, f1[0][l], f1[1][l], f1[2][l], row(ln1_g, l), row(ln1_b, l))
        qt, km, vtm, q1t, q2t, k1, k2, vtd = _inproj(
            xf.reshape(b, s, d), w_in_p[l], row(q_norm_g, l), w_q_p[l], row(kv_norm_g, l), w_kv_p[l], rope)
        o_mla = _mla_attention(qt, km, vtm)
        o_diff = _diff_attention(q1t, q2t, k1, k2, vtd, lamp[l], linit, row(diff_subln_g, l))
        xf = _outproj_ln(xf, o_mla.reshape(b * s, -1), o_diff.reshape(b * s, -1), w_out_p[l],
                         row(ln2_g, l), row(ln2_b, l))
        xf = _ffn_ln(xf, f2[0][l], f2[1][l], f2[2][l], row(ln3_g, l), row(ln3_b, l))
    return xf.reshape(b, s, d)
```

```python
import math

import jax
import jax.numpy as jnp
import numpy as np
from jax import lax
from jax.experimental import pallas as pl
from jax.experimental.pallas import tpu as pltpu

D_MODEL = 1024
DEPTH = 4
MLA_HEADS = 4
MLA_NOPE = 128
MLA_ROPE = 64
MLA_V = 128
MLA_Q_RANK = 256
MLA_KV_RANK = 128
DIFF_HEADS = 4
DIFF_DIM = 64
D_FF = 2816
ROPE_THETA = 10000.0
NORM_EPS = 1e-5
ALPHA = (2 * DEPTH) ** 0.25

LANE = 128
ROW_TILE = 512
KV_CHUNK = ROW_TILE
DIFF_TILE = 1024
MLA_Q_TILE = 1024
MLA_K_TILE = 1024
POS_SPLIT = 128
VMEM_LIMIT = 58 * 1024 * 1024

MLA_QK = MLA_NOPE + 2 * MLA_ROPE
W_IN_COLS = 2048
NEG_INF = float("-inf")
LOG2E = math.log2(math.e)
AUG_ROWS = 16
V_ROWS = MLA_V + AUG_ROWS


def _bf16_pieces(x, n):
    out = []
    for _ in range(n):
        piece = float(np.asarray(x, dtype=jnp.bfloat16).astype(np.float32))
        out.append(piece)
        x -= piece
    return tuple(out)


LOG2E_PIECES = _bf16_pieces(LOG2E, 3)


def _dot(a, b):
    return jnp.dot(a, b, preferred_element_type=jnp.float32)


def _layer_norm(y, g, b):
    mu = jnp.mean(y, axis=-1, keepdims=True)
    d = y - mu
    var = jnp.mean(d * d, axis=-1, keepdims=True)
    return d * lax.rsqrt(var + NORM_EPS) * g + b


def _rms_norm(y, g):
    return y * lax.rsqrt(jnp.mean(y * y, axis=-1, keepdims=True) + NORM_EPS) * g


def _ffn_ln_rows(x, wg_ref, wu_ref, wd_ref, g_ref, b_ref):
    xb = x.astype(jnp.bfloat16)
    gate = _dot(xb, wg_ref[...])
    up = _dot(xb, wu_ref[...])
    h = gate / (1.0 + jnp.exp(-gate)) * up
    ffn = _dot(h.astype(jnp.bfloat16), wd_ref[...])
    return _layer_norm(ALPHA * x + 0.5 * ffn, g_ref[...], b_ref[...])


def _ffn_ln_kernel(x_ref, wg_ref, wu_ref, wd_ref, g_ref, b_ref, o_ref):
    o_ref[...] = _ffn_ln_rows(x_ref[...], wg_ref, wu_ref, wd_ref, g_ref, b_ref)


def _mix_ffn_ln_kernel(x_ref, om_ref, od_ref, wo_ref, g2_ref, b2_ref,
                       wg_ref, wu_ref, wd_ref, g3_ref, b3_ref, o_ref):
    nm = om_ref.shape[1]
    mix = _dot(om_ref[...], wo_ref[:nm, :]) + _dot(od_ref[...], wo_ref[nm:, :])
    x = _layer_norm(ALPHA * x_ref[...] + mix, g2_ref[...], b2_ref[...])
    o_ref[...] = _ffn_ln_rows(x, wg_ref, wu_ref, wd_ref, g3_ref, b3_ref)


def _row_tile(width):
    return pl.BlockSpec((ROW_TILE, width), lambda i: (i, 0))


def _resident(shape):
    return pl.BlockSpec(shape, lambda i: (0, 0), pipeline_mode=pl.Buffered(1))


_FFN_SPECS = [_resident((D_MODEL, D_FF)), _resident((D_MODEL, D_FF)), _resident((D_FF, D_MODEL)),
              _resident((1, D_MODEL)), _resident((1, D_MODEL))]


def _ffn_ln(x, wg, wu, wd, g, b):
    t = x.shape[0]
    return pl.pallas_call(
        _ffn_ln_kernel,
        grid=(t // ROW_TILE,),
        in_specs=[_row_tile(D_MODEL)] + _FFN_SPECS,
        out_specs=_row_tile(D_MODEL),
        out_shape=jax.ShapeDtypeStruct((t, D_MODEL), jnp.float32),
        compiler_params=pltpu.CompilerParams(
            dimension_semantics=("parallel",), vmem_limit_bytes=VMEM_LIMIT),
        name="ffn_ln",
    )(x, wg, wu, wd, g, b)


def _mix_ffn_ln(x, om, od, wo, g2, b2, wg, wu, wd, g3, b3):
    t = x.shape[0]
    nm, nd = om.shape[1], od.shape[1]
    return pl.pallas_call(
        _mix_ffn_ln_kernel,
        grid=(t // ROW_TILE,),
        in_specs=[_row_tile(D_MODEL), _row_tile(nm), _row_tile(nd), _resident((nm + nd, D_MODEL)),
                  _resident((1, D_MODEL)), _resident((1, D_MODEL))] + _FFN_SPECS,
        out_specs=_row_tile(D_MODEL),
        out_shape=jax.ShapeDtypeStruct((t, D_MODEL), jnp.float32),
        compiler_params=pltpu.CompilerParams(
            dimension_semantics=("parallel",), vmem_limit_bytes=VMEM_LIMIT),
        name="mix_ffn_ln",
    )(x, om, od, wo, g2, b2, wg, wu, wd, g3, b3)


def _alibi_slope(h):
    return 2.0 ** (-8.0 * (h + 1) / DIFF_HEADS)


def _inproj_kernel(x_ref, win_ref, qg_ref, wq_ref, kvg_ref, wkv_ref, rope_ref,
                   qt_ref, km_ref, vtm_ref, q1t_ref, q2t_ref, k1_ref, k2_ref, vtd_ref):
    i = pl.program_id(1)
    tm = x_ref.shape[1]
    xb = x_ref[0].astype(jnp.bfloat16)
    h = _dot(xb, win_ref[...])
    c_q = h[:, 0:256]
    c_kv = h[:, 256:384]
    kr = h[:, 384:512]
    dq = h[:, 512:1024]
    dk = h[:, 1024:1536]
    dv = h[:, 1536:2048]
    rope = rope_ref[...]

    q = _dot(_rms_norm(c_q, qg_ref[...]).astype(jnp.bfloat16), wq_ref[...])
    kv = _dot(_rms_norm(c_kv, kvg_ref[...]).astype(jnp.bfloat16), wkv_ref[...])
    krt = kr * rope
    k_rope = krt + pltpu.roll(krt, MLA_ROPE, axis=1)
    mla_scale = (MLA_NOPE + MLA_ROPE) ** -0.5 * LOG2E
    ones_rows = jnp.where(lax.broadcasted_iota(jnp.int32, (AUG_ROWS, tm), 0) == 0, 1.0, 0.0)
    for hh in range(MLA_HEADS):
        qh = q[:, hh * MLA_QK:(hh + 1) * MLA_QK]
        qh = jnp.concatenate([qh[:, :MLA_NOPE], qh[:, MLA_NOPE:] * rope], axis=1) * mla_scale
        qt_ref[0, hh] = qh.T.astype(jnp.bfloat16)
        kvh = kv[:, hh * 256:(hh + 1) * 256]
        km_ref[0, hh] = jnp.concatenate([kvh[:, :MLA_NOPE], k_rope], axis=1).astype(jnp.bfloat16)
        vtm_ref[0, hh, 0] = jnp.concatenate([kvh[:, MLA_NOPE:].T, ones_rows], axis=0).astype(jnp.bfloat16)

    pos_c = i * tm + lax.broadcasted_iota(jnp.int32, (AUG_ROWS, tm), 1)
    row = lax.broadcasted_iota(jnp.int32, (AUG_ROWS, tm), 0)
    hi_c = (pos_c & -POS_SPLIT).astype(jnp.float32)
    lo_c = (pos_c & (POS_SPLIT - 1)).astype(jnp.float32)
    pos_r = i * tm + lax.broadcasted_iota(jnp.int32, (tm, LANE), 0)
    lane = lax.broadcasted_iota(jnp.int32, (tm, LANE), 1)
    hi_r = (pos_r & -POS_SPLIT).astype(jnp.float32)
    lo_r = (pos_r & (POS_SPLIT - 1)).astype(jnp.float32)
    diff_scale = DIFF_DIM ** -0.5 * LOG2E
    zpad = jnp.zeros((DIFF_DIM - AUG_ROWS, tm), jnp.float32)
    def piece(slot):
        return jnp.where(slot < 4, LOG2E_PIECES[0], jnp.where(slot < 8, LOG2E_PIECES[1], LOG2E_PIECES[2]))

    slot_r = lane - DIFF_DIM
    piece_c, term_c = piece(row), row & 3
    piece_r, term_r = piece(slot_r), slot_r & 3
    for hh in range(DIFF_HEADS):
        slope = _alibi_slope(hh)
        qaug = jnp.where(row >= 12, 0.0,
                         jnp.where(term_c == 0, -slope * hi_c,
                                   jnp.where(term_c == 1, -slope * lo_c, piece_c)))
        kaug = jnp.where(slot_r >= 12, 0.0,
                         jnp.where(term_r < 2, piece_r,
                                   jnp.where(term_r == 2, slope * hi_r, slope * lo_r)))
        tq = (dq[:, hh * 128:(hh + 1) * 128] * diff_scale).T
        q1t_ref[0, hh] = jnp.concatenate([tq[:DIFF_DIM], qaug, zpad], axis=0).astype(jnp.bfloat16)
        q2t_ref[0, hh] = jnp.concatenate([tq[DIFF_DIM:], qaug, zpad], axis=0).astype(jnp.bfloat16)
        tk = dk[:, hh * 128:(hh + 1) * 128]
        k1_ref[0, hh] = jnp.where(lane < DIFF_DIM, tk, kaug).astype(jnp.bfloat16)
        k2_ref[0, hh] = jnp.where(lane < DIFF_DIM, pltpu.roll(tk, DIFF_DIM, axis=1), kaug).astype(jnp.bfloat16)
        vtd_ref[0, hh, 0] = jnp.concatenate(
            [dv[:, hh * 128:(hh + 1) * 128].T, ones_rows], axis=0).astype(jnp.bfloat16)


def _inproj(x, win, qg, wq, kvg, wkv, rope):
    b, s, _ = x.shape
    nc = s // KV_CHUNK
    tm = ROW_TILE
    bf = jnp.bfloat16
    const = lambda shape: pl.BlockSpec(shape, lambda bb, i: (0,) * len(shape))
    out_shape = (
        jax.ShapeDtypeStruct((b, MLA_HEADS, MLA_QK, s), bf),
        jax.ShapeDtypeStruct((b, MLA_HEADS, s, MLA_QK), bf),
        jax.ShapeDtypeStruct((b, MLA_HEADS, nc, V_ROWS, KV_CHUNK), bf),
        jax.ShapeDtypeStruct((b, DIFF_HEADS, LANE, s), bf),
        jax.ShapeDtypeStruct((b, DIFF_HEADS, LANE, s), bf),
        jax.ShapeDtypeStruct((b, DIFF_HEADS, s, LANE), bf),
        jax.ShapeDtypeStruct((b, DIFF_HEADS, s, LANE), bf),
        jax.ShapeDtypeStruct((b, DIFF_HEADS, nc, V_ROWS, KV_CHUNK), bf),
    )
    out_specs = (
        pl.BlockSpec((1, MLA_HEADS, MLA_QK, tm), lambda bb, i: (bb, 0, 0, i)),
        pl.BlockSpec((1, MLA_HEADS, tm, MLA_QK), lambda bb, i: (bb, 0, i, 0)),
        pl.BlockSpec((1, MLA_HEADS, 1, V_ROWS, tm), lambda bb, i: (bb, 0, i, 0, 0)),
        pl.BlockSpec((1, DIFF_HEADS, LANE, tm), lambda bb, i: (bb, 0, 0, i)),
        pl.BlockSpec((1, DIFF_HEADS, LANE, tm), lambda bb, i: (bb, 0, 0, i)),
        pl.BlockSpec((1, DIFF_HEADS, tm, LANE), lambda bb, i: (bb, 0, i, 0)),
        pl.BlockSpec((1, DIFF_HEADS, tm, LANE), lambda bb, i: (bb, 0, i, 0)),
        pl.BlockSpec((1, DIFF_HEADS, 1, V_ROWS, tm), lambda bb, i: (bb, 0, i, 0, 0)),
    )
    return pl.pallas_call(
        _inproj_kernel,
        grid=(b, s // tm),
        in_specs=[
            pl.BlockSpec((1, tm, D_MODEL), lambda bb, i: (bb, i, 0)),
            const((D_MODEL, W_IN_COLS)),
            const((1, MLA_Q_RANK)),
            const((MLA_Q_RANK, MLA_HEADS * MLA_QK)),
            const((1, MLA_KV_RANK)),
            const((MLA_KV_RANK, MLA_HEADS * 256)),
            pl.BlockSpec((tm, LANE), lambda bb, i: (i, 0)),
        ],
        out_specs=out_specs,
        out_shape=out_shape,
        compiler_params=pltpu.CompilerParams(
            dimension_semantics=("parallel", "parallel"), vmem_limit_bytes=VMEM_LIMIT),
        name="inproj",
    )(x, win, qg, wq, kvg, wkv, rope)


def _qk_stage(k_tile, q_tile, s_ref, mb_ref, corr=None):
    s = _dot(k_tile, q_tile)
    if corr is not None:
        s = s - corr
    s_ref[...] = s
    mb_ref[...] = jnp.max(s, axis=0, keepdims=True)


def _sm_stage(s_ref, mb_ref, p_ref, m_ref):
    m_old = m_ref[...]
    m_new = jnp.maximum(m_old, mb_ref[...])
    p_ref[...] = jnp.exp2(s_ref[...] - m_new).astype(p_ref.dtype)
    m_ref[...] = m_new
    return jnp.exp2(m_old - m_new)


def _pv_stage(vt_tiles, p_ref, acc_ref, alpha):
    acc = acc_ref[...]
    for n, vt_tile in enumerate(vt_tiles):
        acc = acc + _dot(vt_tile, p_ref[n * KV_CHUNK:(n + 1) * KV_CHUNK, :])
    acc_ref[...] = acc if alpha is None else alpha * acc


def _key_rows(c, tk):
    return pl.ds(pl.multiple_of(c * tk, tk), tk)


def _mla_kernel(qt_ref, k_ref, vt_ref, o_ref, s0, s1, p0, p1, mb0, mb1, m_sc, acc_sc):
    tk = s0.shape[0]
    sub = tk // KV_CHUNK
    nk = vt_ref.shape[2] // sub
    s_bufs, p_bufs, mb_bufs = (s0, s1), (p0, p1), (mb0, mb1)
    m_sc[...] = jnp.full_like(m_sc, NEG_INF)
    acc_sc[...] = jnp.zeros_like(acc_sc)

    tq = s0.shape[1]
    halves = [pl.ds(i * (tq // 2), tq // 2) for i in range(2)]

    def qk(c, slot):
        for hv in halves:
            _qk_stage(k_ref[0, 0, _key_rows(c, tk), :], qt_ref[0, 0, :, hv],
                      s_bufs[slot].at[:, hv], mb_bufs[slot].at[:, hv])

    def sm(slot):
        return [_sm_stage(s_bufs[slot].at[:, hv], mb_bufs[slot].at[:, hv], p_bufs[slot].at[:, hv],
                          m_sc.at[:, hv]) for hv in halves]

    def pv(c, slot, alphas):
        vt = [vt_ref[0, 0, c * sub + n] for n in range(sub)]
        for i, hv in enumerate(halves):
            _pv_stage(vt, p_bufs[slot].at[:, hv], acc_sc.at[:, hv], None if alphas is None else alphas[i])

    def step(c, slot):
        qk(c + 1, 1 - slot)
        alpha = sm(slot)
        pv(c - 1, 1 - slot, alpha)

    qk(0, 0)
    qk(1, 1)
    sm(0)

    def pair(t, carry):
        step(2 * t + 1, 1)
        step(2 * t + 2, 0)
        return carry

    lax.fori_loop(0, nk // 2 - 1, pair, 0)
    alpha = sm(1)
    pv(nk - 2, 0, alpha)
    pv(nk - 1, 1, None)
    acc = acc_sc[...]
    o = acc[:MLA_V] / acc[MLA_V:MLA_V + 1]
    o_ref[0] = o.T.astype(o_ref.dtype)


def _mla_attention(qt, k, vt):
    b, hds, _, s = qt.shape
    nc = vt.shape[2]
    tq, tk = MLA_Q_TILE, MLA_K_TILE
    assert tk % KV_CHUNK == 0 and s % (2 * tk) == 0
    f32, bf = jnp.float32, jnp.bfloat16
    return pl.pallas_call(
        _mla_kernel,
        grid=(b, hds, s // tq),
        in_specs=[
            pl.BlockSpec((1, 1, MLA_QK, tq), lambda bb, h, i: (bb, h, 0, i)),
            pl.BlockSpec((1, 1, s, MLA_QK), lambda bb, h, i: (bb, h, 0, 0)),
            pl.BlockSpec((1, 1, nc, V_ROWS, KV_CHUNK), lambda bb, h, i: (bb, h, 0, 0, 0)),
        ],
        out_specs=pl.BlockSpec((1, tq, MLA_V), lambda bb, h, i: (bb, i, h)),
        out_shape=jax.ShapeDtypeStruct((b, s, hds * MLA_V), bf),
        scratch_shapes=[
            pltpu.VMEM((tk, tq), f32), pltpu.VMEM((tk, tq), f32),
            pltpu.VMEM((tk, tq), bf), pltpu.VMEM((tk, tq), bf),
            pltpu.VMEM((1, tq), f32), pltpu.VMEM((1, tq), f32),
            pltpu.VMEM((1, tq), f32),
            pltpu.VMEM((V_ROWS, tq), f32),
        ],
        compiler_params=pltpu.CompilerParams(
            dimension_semantics=("parallel", "parallel", "arbitrary"), vmem_limit_bytes=VMEM_LIMIT),
        name="mla_attn",
    )(qt, k, vt)


def _diff_kernel(q1t_ref, q2t_ref, k1_ref, k2_ref, vt_ref, lam_ref, linit_ref, g_ref, o_ref,
                 qv_sc, corr_sc, s_sc, p_sc, mb_sc, m_sc, acc_sc):
    h = pl.program_id(1)
    qi = pl.program_id(2)
    tq = q1t_ref.shape[3]
    sub = tq // KV_CHUNK
    nk = vt_ref.shape[2] // sub
    k_refs = (k1_ref, k2_ref)
    m_sc[...] = jnp.full_like(m_sc, NEG_INF)
    acc_sc[...] = jnp.zeros_like(acc_sc)
    row = lax.broadcasted_iota(jnp.int32, (LANE, tq), 0)
    for mi, qref in enumerate((q1t_ref, q2t_ref)):
        qq = qref[0, 0]
        qv_sc[mi] = qq
        qv_sc[2 + mi] = jnp.where(row < DIFF_DIM, qq, -qq)

    @pl.when(qi == 0)
    def _():
        jj = lax.broadcasted_iota(jnp.int32, (tq, tq), 0)
        ii = lax.broadcasted_iota(jnp.int32, (tq, tq), 1)
        expo = jnp.full((1, tq), 127 + 1, jnp.int32) - 2 * (h + 1)
        slope2 = lax.bitcast_convert_type(expo << 23, jnp.float32)
        corr_sc[...] = (slope2 * LOG2E) * jnp.maximum(jj - ii, 0).astype(jnp.float32)

    def chunk(j):
        return lax.rem(qi + j, nk)

    def qk(j, slot, diagonal=False):
        c = chunk(j)
        variant = 0 if diagonal else jnp.where(c > qi, 2, 0)
        for mi in range(2):
            _qk_stage(k_refs[mi][0, 0, _key_rows(c, tq), :], qv_sc[variant + mi],
                      s_sc.at[2 * slot + mi], mb_sc.at[2 * slot + mi],
                      corr_sc[...] if diagonal else None)

    def sm(slot):
        return [_sm_stage(s_sc.at[2 * slot + mi], mb_sc.at[2 * slot + mi], p_sc.at[2 * slot + mi],
                          m_sc.at[mi]) for mi in range(2)]

    def pv(j, slot, alphas):
        c = chunk(j)
        vt = [vt_ref[0, 0, c * sub + n] for n in range(sub)]
        for mi in range(2):
            _pv_stage(vt, p_sc.at[2 * slot + mi], acc_sc.at[mi], alphas[mi])

    def step(j, slot):
        qk(j + 1, 1 - slot)
        alphas = sm(slot)
        pv(j - 1, 1 - slot, alphas)

    qk(0, 0, diagonal=True)
    qk(1, 1)
    sm(0)

    def pair(t, carry):
        step(2 * t + 1, 1)
        step(2 * t + 2, 0)
        return carry

    lax.fori_loop(0, nk // 2 - 1, pair, 0)
    alphas = sm(1)
    pv(nk - 2, 0, alphas)
    pv(nk - 1, 1, (None, None))

    lamp = lam_ref[...]
    linit = linit_ref[...]
    lam = (jnp.exp(jnp.sum(lamp[0:1] * lamp[1:2], axis=-1, keepdims=True))
           - jnp.exp(jnp.sum(lamp[2:3] * lamp[3:4], axis=-1, keepdims=True)) + linit)
    nv = 2 * DIFF_DIM
    a1, a2 = acc_sc[0], acc_sc[1]
    o = a1[:nv] / a1[nv:nv + 1] - lam * (a2[:nv] / a2[nv:nv + 1])
    o = _rms_norm(o.T, g_ref[...]) * (1.0 - linit)
    o_ref[0] = o.astype(o_ref.dtype)


def _diff_attention(q1t, q2t, k1, k2, vt, lamp, linit, subln_g):
    b, hds, _, s = q1t.shape
    nc = vt.shape[2]
    t = DIFF_TILE
    assert t % KV_CHUNK == 0 and s % (2 * t) == 0
    f32, bf = jnp.float32, jnp.bfloat16
    qspec = pl.BlockSpec((1, 1, LANE, t), lambda bb, h, i: (bb, h, 0, i))
    kspec = pl.BlockSpec((1, 1, s, LANE), lambda bb, h, i: (bb, h, 0, 0))
    const = lambda shape: pl.BlockSpec(shape, lambda bb, h, i: (0,) * len(shape))
    return pl.pallas_call(
        _diff_kernel,
        grid=(b, hds, s // t),
        in_specs=[
            qspec, qspec, kspec, kspec,
            pl.BlockSpec((1, 1, nc, V_ROWS, KV_CHUNK), lambda bb, h, i: (bb, h, 0, 0, 0)),
            const((4, DIFF_DIM)), const((1, 1)), const((1, 2 * DIFF_DIM)),
        ],
        out_specs=pl.BlockSpec((1, t, 2 * DIFF_DIM), lambda bb, h, i: (bb, i, h)),
        out_shape=jax.ShapeDtypeStruct((b, s, hds * 2 * DIFF_DIM), bf),
        scratch_shapes=[
            pltpu.VMEM((4, LANE, t), bf),
            pltpu.VMEM((t, t), f32),
            pltpu.VMEM((4, t, t), f32),
            pltpu.VMEM((4, t, t), bf),
            pltpu.VMEM((4, 1, t), f32),
            pltpu.VMEM((2, 1, t), f32),
            pltpu.VMEM((2, V_ROWS, t), f32),
        ],
        compiler_params=pltpu.CompilerParams(
            dimension_semantics=("parallel", "parallel", "arbitrary"), vmem_limit_bytes=VMEM_LIMIT),
        name="diff_attn",
    )(q1t, q2t, k1, k2, vt, lamp, linit, subln_g)


def _swap_halves(w):
    half = w.shape[-1] // 2
    return jnp.concatenate([w[..., half:], w[..., :half]], axis=-1)


def _prep_w_in(w_in):
    c_q = w_in[..., :256]
    c_kv = w_in[..., 256:384]
    k_rope = w_in[..., 384:448]
    rest = w_in[..., 448:]
    return jnp.concatenate([c_q, c_kv, k_rope, _swap_halves(k_rope), rest], axis=-1).astype(jnp.bfloat16)


def _prep_w_q_up(w_q_up):
    l, r, _ = w_q_up.shape
    w = w_q_up.reshape(l, r, MLA_HEADS, MLA_NOPE + MLA_ROPE)
    rope = w[..., MLA_NOPE:]
    w = jnp.concatenate([w[..., :MLA_NOPE], rope, _swap_halves(rope)], axis=-1)
    return w.reshape(l, r, MLA_HEADS * MLA_QK).astype(jnp.bfloat16)


def _rope_table(s):
    inv_freq = ROPE_THETA ** (-jnp.arange(0, MLA_ROPE, 2, dtype=jnp.float32) / MLA_ROPE)
    ang = jnp.arange(s, dtype=jnp.int32).astype(jnp.float32)[:, None] * inv_freq[None, :]
    cos, sin = jnp.cos(ang), jnp.sin(ang)
    return jnp.concatenate([cos, cos, -sin, sin], axis=-1)


def kernel(x, ffn1_w_gate, ffn1_w_up, ffn1_w_down, ln1_g, ln1_b, w_in, q_norm_g, w_q_up, kv_norm_g, w_kv_up, diff_lambda_q1, diff_lambda_k1, diff_lambda_q2, diff_lambda_k2, diff_subln_g, w_out, ln2_g, ln2_b, ffn2_w_gate, ffn2_w_up, ffn2_w_down, ln3_g, ln3_b):
    b, s, d = x.shape
    bf = jnp.bfloat16
    rope = _rope_table(s)
    w_in_p = _prep_w_in(w_in)
    w_q_p = _prep_w_q_up(w_q_up)
    w_kv_p = w_kv_up.astype(bf)
    w_out_p = w_out.astype(bf)
    f1 = (ffn1_w_gate.astype(bf), ffn1_w_up.astype(bf), ffn1_w_down.astype(bf))
    f2 = (ffn2_w_gate.astype(bf), ffn2_w_up.astype(bf), ffn2_w_down.astype(bf))
    lamp = jnp.stack([diff_lambda_q1, diff_lambda_k1, diff_lambda_q2, diff_lambda_k2], axis=1)
    row = lambda v, l: v[l][None, :]

    xf = x.reshape(b * s, d)
    for l in range(DEPTH):
        linit = jnp.full((1, 1), 0.8 - 0.6 * math.exp(-0.3 * l), jnp.float32)
        xf = _ffn_ln(xf, f1[0][l], f1[1][l], f1[2][l], row(ln1_g, l), row(ln1_b, l))
        qt, km, vtm, q1t, q2t, k1, k2, vtd = _inproj(
            xf.reshape(b, s, d), w_in_p[l], row(q_norm_g, l), w_q_p[l], row(kv_norm_g, l), w_kv_p[l], rope)
        o_mla = _mla_attention(qt, km, vtm)
        o_diff = _diff_attention(q1t, q2t, k1, k2, vtd, lamp[l], linit, row(diff_subln_g, l))
        xf = _mix_ffn_ln(xf, o_mla.reshape(b * s, -1), o_diff.reshape(b * s, -1), w_out_p[l],
                         row(ln2_g, l), row(ln2_b, l),
                         f2[0][l], f2[1][l], f2[2][l], row(ln3_g, l), row(ln3_b, l))
    return xf.reshape(b, s, d)
```

```python
import math

import jax
import jax.numpy as jnp
import numpy as np
from jax import lax
from jax.experimental import pallas as pl
from jax.experimental.pallas import tpu as pltpu

D_MODEL = 1024
DEPTH = 4
MLA_HEADS = 4
MLA_NOPE = 128
MLA_ROPE = 64
MLA_V = 128
MLA_Q_RANK = 256
MLA_KV_RANK = 128
DIFF_HEADS = 4
DIFF_DIM = 64
D_FF = 2816
ROPE_THETA = 10000.0
NORM_EPS = 1e-5
ALPHA = (2 * DEPTH) ** 0.25

LANE = 128
ROW_TILE = 512
KV_CHUNK = ROW_TILE
DIFF_TILE = 1024
MLA_Q_TILE = 2048
MLA_K_TILE = 1024
POS_SPLIT = 128
VMEM_LIMIT = 58 * 1024 * 1024

MLA_QK = MLA_NOPE + 2 * MLA_ROPE
W_IN_COLS = 2048
NEG_INF = float("-inf")
LOG2E = math.log2(math.e)
AUG_ROWS = 16
V_ROWS = MLA_V + AUG_ROWS


def _bf16_pieces(x, n):
    out = []
    for _ in range(n):
        piece = float(np.asarray(x, dtype=jnp.bfloat16).astype(np.float32))
        out.append(piece)
        x -= piece
    return tuple(out)


LOG2E_PIECES = _bf16_pieces(LOG2E, 3)


def _dot(a, b):
    return jnp.dot(a, b, preferred_element_type=jnp.float32)


def _layer_norm(y, g, b):
    mu = jnp.mean(y, axis=-1, keepdims=True)
    d = y - mu
    var = jnp.mean(d * d, axis=-1, keepdims=True)
    return d * lax.rsqrt(var + NORM_EPS) * g + b


def _rms_norm(y, g):
    return y * lax.rsqrt(jnp.mean(y * y, axis=-1, keepdims=True) + NORM_EPS) * g


def _ffn_ln_rows(x, wg_ref, wu_ref, wd_ref, g_ref, b_ref):
    xb = x.astype(jnp.bfloat16)
    gate = _dot(xb, wg_ref[...])
    up = _dot(xb, wu_ref[...])
    h = gate / (1.0 + jnp.exp(-gate)) * up
    ffn = _dot(h.astype(jnp.bfloat16), wd_ref[...])
    return _layer_norm(ALPHA * x + 0.5 * ffn, g_ref[...], b_ref[...])


def _ffn_ln_kernel(x_ref, wg_ref, wu_ref, wd_ref, g_ref, b_ref, o_ref):
    o_ref[...] = _ffn_ln_rows(x_ref[...], wg_ref, wu_ref, wd_ref, g_ref, b_ref)


def _mix_ffn_ln_kernel(x_ref, om_ref, od_ref, wo_ref, g2_ref, b2_ref,
                       wg_ref, wu_ref, wd_ref, g3_ref, b3_ref, o_ref):
    nm = om_ref.shape[1]
    mix = _dot(om_ref[...], wo_ref[:nm, :]) + _dot(od_ref[...], wo_ref[nm:, :])
    x = _layer_norm(ALPHA * x_ref[...] + mix, g2_ref[...], b2_ref[...])
    o_ref[...] = _ffn_ln_rows(x, wg_ref, wu_ref, wd_ref, g3_ref, b3_ref)


def _row_tile(width):
    return pl.BlockSpec((ROW_TILE, width), lambda i: (i, 0))


def _layer_block(shape, layer):
    return pl.BlockSpec((None,) + tuple(shape), lambda *_: (layer,) + (0,) * len(shape),
                        pipeline_mode=pl.Buffered(1))


def _ffn_specs(layer):
    return [_layer_block((D_MODEL, D_FF), layer), _layer_block((D_MODEL, D_FF), layer),
            _layer_block((D_FF, D_MODEL), layer), _layer_block((1, D_MODEL), layer), _layer_block((1, D_MODEL), layer)]


def _ffn_ln(layer, x, wg, wu, wd, g, b):
    t = x.shape[0]
    return pl.pallas_call(
        _ffn_ln_kernel,
        grid=(t // ROW_TILE,),
        in_specs=[_row_tile(D_MODEL)] + _ffn_specs(layer),
        out_specs=_row_tile(D_MODEL),
        out_shape=jax.ShapeDtypeStruct((t, D_MODEL), jnp.float32),
        compiler_params=pltpu.CompilerParams(
            dimension_semantics=("parallel",), vmem_limit_bytes=VMEM_LIMIT),
        name="ffn_ln",
    )(x, wg, wu, wd, g, b)


def _mix_ffn_ln(layer, x, om, od, wo, g2, b2, wg, wu, wd, g3, b3):
    t = x.shape[0]
    nm, nd = om.shape[1], od.shape[1]
    return pl.pallas_call(
        _mix_ffn_ln_kernel,
        grid=(t // ROW_TILE,),
        in_specs=[_row_tile(D_MODEL), _row_tile(nm), _row_tile(nd), _layer_block((nm + nd, D_MODEL), layer),
                  _layer_block((1, D_MODEL), layer), _layer_block((1, D_MODEL), layer)] + _ffn_specs(layer),
        out_specs=_row_tile(D_MODEL),
        out_shape=jax.ShapeDtypeStruct((t, D_MODEL), jnp.float32),
        compiler_params=pltpu.CompilerParams(
            dimension_semantics=("parallel",), vmem_limit_bytes=VMEM_LIMIT),
        name="mix_ffn_ln",
    )(x, om, od, wo, g2, b2, wg, wu, wd, g3, b3)


def _alibi_slope(h):
    return 2.0 ** (-8.0 * (h + 1) / DIFF_HEADS)


def _inproj_kernel(x_ref, win_ref, qg_ref, wq_ref, kvg_ref, wkv_ref, rope_ref,
                   qt_ref, km_ref, vtm_ref, q1t_ref, q2t_ref, k1_ref, k2_ref, vtd_ref):
    i = pl.program_id(1)
    tm = x_ref.shape[1]
    xb = x_ref[0].astype(jnp.bfloat16)
    h = _dot(xb, win_ref[...])
    c_q = h[:, 0:256]
    c_kv = h[:, 256:384]
    kr = h[:, 384:512]
    dq = h[:, 512:1024]
    dk = h[:, 1024:1536]
    dv = h[:, 1536:2048]
    rope = rope_ref[...]

    q = _dot(_rms_norm(c_q, qg_ref[...]).astype(jnp.bfloat16), wq_ref[...])
    kv = _dot(_rms_norm(c_kv, kvg_ref[...]).astype(jnp.bfloat16), wkv_ref[...])
    krt = kr * rope
    k_rope = krt + pltpu.roll(krt, MLA_ROPE, axis=1)
    mla_scale = (MLA_NOPE + MLA_ROPE) ** -0.5 * LOG2E
    ones_rows = jnp.where(lax.broadcasted_iota(jnp.int32, (AUG_ROWS, tm), 0) == 0, 1.0, 0.0)
    for hh in range(MLA_HEADS):
        qh = q[:, hh * MLA_QK:(hh + 1) * MLA_QK]
        qh = jnp.concatenate([qh[:, :MLA_NOPE], qh[:, MLA_NOPE:] * rope], axis=1) * mla_scale
        qt_ref[0, hh] = qh.T.astype(jnp.bfloat16)
        kvh = kv[:, hh * 256:(hh + 1) * 256]
        km_ref[0, hh] = jnp.concatenate([kvh[:, :MLA_NOPE], k_rope], axis=1).astype(jnp.bfloat16)
        vtm_ref[0, hh, 0] = jnp.concatenate([kvh[:, MLA_NOPE:].T, ones_rows], axis=0).astype(jnp.bfloat16)

    pos_c = i * tm + lax.broadcasted_iota(jnp.int32, (AUG_ROWS, tm), 1)
    row = lax.broadcasted_iota(jnp.int32, (AUG_ROWS, tm), 0)
    hi_c = (pos_c & -POS_SPLIT).astype(jnp.float32)
    lo_c = (pos_c & (POS_SPLIT - 1)).astype(jnp.float32)
    pos_r = i * tm + lax.broadcasted_iota(jnp.int32, (tm, LANE), 0)
    lane = lax.broadcasted_iota(jnp.int32, (tm, LANE), 1)
    hi_r = (pos_r & -POS_SPLIT).astype(jnp.float32)
    lo_r = (pos_r & (POS_SPLIT - 1)).astype(jnp.float32)
    diff_scale = DIFF_DIM ** -0.5 * LOG2E
    zpad = jnp.zeros((DIFF_DIM - AUG_ROWS, tm), jnp.float32)
    def piece(slot):
        return jnp.where(slot < 4, LOG2E_PIECES[0], jnp.where(slot < 8, LOG2E_PIECES[1], LOG2E_PIECES[2]))

    slot_r = lane - DIFF_DIM
    piece_c, term_c = piece(row), row & 3
    piece_r, term_r = piece(slot_r), slot_r & 3
    for hh in range(DIFF_HEADS):
        slope = _alibi_slope(hh)
        qaug = jnp.where(row >= 12, 0.0,
                         jnp.where(term_c == 0, -slope * hi_c,
                                   jnp.where(term_c == 1, -slope * lo_c, piece_c)))
        kaug = jnp.where(slot_r >= 12, 0.0,
                         jnp.where(term_r < 2, piece_r,
                                   jnp.where(term_r == 2, slope * hi_r, slope * lo_r)))
        tq = (dq[:, hh * 128:(hh + 1) * 128] * diff_scale).T
        q1t_ref[0, hh] = jnp.concatenate([tq[:DIFF_DIM], qaug, zpad], axis=0).astype(jnp.bfloat16)
        q2t_ref[0, hh] = jnp.concatenate([tq[DIFF_DIM:], qaug, zpad], axis=0).astype(jnp.bfloat16)
        tk = dk[:, hh * 128:(hh + 1) * 128]
        k1_ref[0, hh] = jnp.where(lane < DIFF_DIM, tk, kaug).astype(jnp.bfloat16)
        k2_ref[0, hh] = jnp.where(lane < DIFF_DIM, pltpu.roll(tk, DIFF_DIM, axis=1), kaug).astype(jnp.bfloat16)
        vtd_ref[0, hh, 0] = jnp.concatenate(
            [dv[:, hh * 128:(hh + 1) * 128].T, ones_rows], axis=0).astype(jnp.bfloat16)


def _inproj(layer, x, win, qg, wq, kvg, wkv, rope):
    b, s, _ = x.shape
    nc = s // KV_CHUNK
    tm = ROW_TILE
    bf = jnp.bfloat16
    const = lambda shape: _layer_block(shape, layer)
    out_shape = (
        jax.ShapeDtypeStruct((b, MLA_HEADS, MLA_QK, s), bf),
        jax.ShapeDtypeStruct((b, MLA_HEADS, s, MLA_QK), bf),
        jax.ShapeDtypeStruct((b, MLA_HEADS, nc, V_ROWS, KV_CHUNK), bf),
        jax.ShapeDtypeStruct((b, DIFF_HEADS, LANE, s), bf),
        jax.ShapeDtypeStruct((b, DIFF_HEADS, LANE, s), bf),
        jax.ShapeDtypeStruct((b, DIFF_HEADS, s, LANE), bf),
        jax.ShapeDtypeStruct((b, DIFF_HEADS, s, LANE), bf),
        jax.ShapeDtypeStruct((b, DIFF_HEADS, nc, V_ROWS, KV_CHUNK), bf),
    )
    out_specs = (
        pl.BlockSpec((1, MLA_HEADS, MLA_QK, tm), lambda bb, i: (bb, 0, 0, i)),
        pl.BlockSpec((1, MLA_HEADS, tm, MLA_QK), lambda bb, i: (bb, 0, i, 0)),
        pl.BlockSpec((1, MLA_HEADS, 1, V_ROWS, tm), lambda bb, i: (bb, 0, i, 0, 0)),
        pl.BlockSpec((1, DIFF_HEADS, LANE, tm), lambda bb, i: (bb, 0, 0, i)),
        pl.BlockSpec((1, DIFF_HEADS, LANE, tm), lambda bb, i: (bb, 0, 0, i)),
        pl.BlockSpec((1, DIFF_HEADS, tm, LANE), lambda bb, i: (bb, 0, i, 0)),
        pl.BlockSpec((1, DIFF_HEADS, tm, LANE), lambda bb, i: (bb, 0, i, 0)),
        pl.BlockSpec((1, DIFF_HEADS, 1, V_ROWS, tm), lambda bb, i: (bb, 0, i, 0, 0)),
    )
    return pl.pallas_call(
        _inproj_kernel,
        grid=(b, s // tm),
        in_specs=[
            pl.BlockSpec((1, tm, D_MODEL), lambda bb, i: (bb, i, 0)),
            const((D_MODEL, W_IN_COLS)),
            const((1, MLA_Q_RANK)),
            const((MLA_Q_RANK, MLA_HEADS * MLA_QK)),
            const((1, MLA_KV_RANK)),
            const((MLA_KV_RANK, MLA_HEADS * 256)),
            pl.BlockSpec((tm, LANE), lambda bb, i: (i, 0)),
        ],
        out_specs=out_specs,
        out_shape=out_shape,
        compiler_params=pltpu.CompilerParams(
            dimension_semantics=("parallel", "parallel"), vmem_limit_bytes=VMEM_LIMIT),
        name="inproj",
    )(x, win, qg, wq, kvg, wkv, rope)


def _qk_stage(k_tile, q_tile, s_ref, mb_ref, corr=None):
    s = _dot(k_tile, q_tile)
    if corr is not None:
        s = s - corr
    s_ref[...] = s
    mb_ref[...] = jnp.max(s, axis=0, keepdims=True)


def _sm_stage(s_ref, mb_ref, p_ref, m_ref):
    m_old = m_ref[...]
    m_new = jnp.maximum(m_old, mb_ref[...])
    p_ref[...] = jnp.exp2(s_ref[...] - m_new).astype(p_ref.dtype)
    m_ref[...] = m_new
    return jnp.exp2(m_old - m_new)


def _pv_stage(vt_tiles, p_ref, acc_ref, alpha):
    acc = acc_ref[...]
    for n, vt_tile in enumerate(vt_tiles):
        acc = acc + _dot(vt_tile, p_ref[n * KV_CHUNK:(n + 1) * KV_CHUNK, :])
    acc_ref[...] = acc if alpha is None else alpha * acc


def _key_rows(c, tk):
    return pl.ds(pl.multiple_of(c * tk, tk), tk)


def _mla_kernel(qt_ref, k_ref, vt_ref, o_ref, s0, s1, p0, p1, mb0, mb1, m_sc, acc_sc):
    tk = s0.shape[0]
    sub = tk // KV_CHUNK
    nk = vt_ref.shape[2] // sub
    s_bufs, p_bufs, mb_bufs = (s0, s1), (p0, p1), (mb0, mb1)
    m_sc[...] = jnp.full_like(m_sc, NEG_INF)
    acc_sc[...] = jnp.zeros_like(acc_sc)

    tq = s0.shape[1]
    halves = [pl.ds(i * (tq // 2), tq // 2) for i in range(2)]

    def qk(c, slot):
        for hv in halves:
            _qk_stage(k_ref[0, 0, _key_rows(c, tk), :], qt_ref[0, 0, :, hv],
                      s_bufs[slot].at[:, hv], mb_bufs[slot].at[:, hv])

    def sm(slot):
        return [_sm_stage(s_bufs[slot].at[:, hv], mb_bufs[slot].at[:, hv], p_bufs[slot].at[:, hv],
                          m_sc.at[:, hv]) for hv in halves]

    def pv(c, slot, alphas):
        vt = [vt_ref[0, 0, c * sub + n] for n in range(sub)]
        for i, hv in enumerate(halves):
            _pv_stage(vt, p_bufs[slot].at[:, hv], acc_sc.at[:, hv], None if alphas is None else alphas[i])

    def step(c, slot):
        qk(c + 1, 1 - slot)
        alpha = sm(slot)
        pv(c - 1, 1 - slot, alpha)

    qk(0, 0)
    qk(1, 1)
    sm(0)

    def pair(t, carry):
        step(2 * t + 1, 1)
        step(2 * t + 2, 0)
        return carry

    lax.fori_loop(0, nk // 2 - 1, pair, 0)
    alpha = sm(1)
    pv(nk - 2, 0, alpha)
    pv(nk - 1, 1, None)
    acc = acc_sc[...]
    o = acc[:MLA_V] / acc[MLA_V:MLA_V + 1]
    o_ref[0] = o.T.astype(o_ref.dtype)


def _mla_attention(qt, k, vt):
    b, hds, _, s = qt.shape
    nc = vt.shape[2]
    tq, tk = MLA_Q_TILE, MLA_K_TILE
    assert tk % KV_CHUNK == 0 and s % (2 * tk) == 0
    f32, bf = jnp.float32, jnp.bfloat16
    return pl.pallas_call(
        _mla_kernel,
        grid=(b, hds, s // tq),
        in_specs=[
            pl.BlockSpec((1, 1, MLA_QK, tq), lambda bb, h, i: (bb, h, 0, i)),
            pl.BlockSpec((1, 1, s, MLA_QK), lambda bb, h, i: (bb, h, 0, 0)),
            pl.BlockSpec((1, 1, nc, V_ROWS, KV_CHUNK), lambda bb, h, i: (bb, h, 0, 0, 0)),
        ],
        out_specs=pl.BlockSpec((1, tq, MLA_V), lambda bb, h, i: (bb, i, h)),
        out_shape=jax.ShapeDtypeStruct((b, s, hds * MLA_V), bf),
        scratch_shapes=[
            pltpu.VMEM((tk, tq), f32), pltpu.VMEM((tk, tq), f32),
            pltpu.VMEM((tk, tq), bf), pltpu.VMEM((tk, tq), bf),
            pltpu.VMEM((1, tq), f32), pltpu.VMEM((1, tq), f32),
            pltpu.VMEM((1, tq), f32),
            pltpu.VMEM((V_ROWS, tq), f32),
        ],
        compiler_params=pltpu.CompilerParams(
            dimension_semantics=("parallel", "parallel", "arbitrary"), vmem_limit_bytes=VMEM_LIMIT),
        name="mla_attn",
    )(qt, k, vt)


def _diff_kernel(q1t_ref, q2t_ref, k1_ref, k2_ref, vt_ref, lam_ref, linit_ref, g_ref, o_ref,
                 qv_sc, corr_sc, s_sc, p_sc, mb_sc, m_sc, acc_sc):
    h = pl.program_id(1)
    qi = pl.program_id(2)
    tq = q1t_ref.shape[3]
    sub = tq // KV_CHUNK
    nk = vt_ref.shape[2] // sub
    k_refs = (k1_ref, k2_ref)
    m_sc[...] = jnp.full_like(m_sc, NEG_INF)
    acc_sc[...] = jnp.zeros_like(acc_sc)
    row = lax.broadcasted_iota(jnp.int32, (LANE, tq), 0)
    for mi, qref in enumerate((q1t_ref, q2t_ref)):
        qq = qref[0, 0]
        qv_sc[mi] = qq
        qv_sc[2 + mi] = jnp.where(row < DIFF_DIM, qq, -qq)

    @pl.when(qi == 0)
    def _():
        jj = lax.broadcasted_iota(jnp.int32, (tq, tq), 0)
        ii = lax.broadcasted_iota(jnp.int32, (tq, tq), 1)
        expo = jnp.full((1, tq), 127 + 1, jnp.int32) - 2 * (h + 1)
        slope2 = lax.bitcast_convert_type(expo << 23, jnp.float32)
        corr_sc[...] = (slope2 * LOG2E) * jnp.maximum(jj - ii, 0).astype(jnp.float32)

    def chunk(j):
        return lax.rem(qi + j, nk)

    def qk(j, slot, diagonal=False):
        c = chunk(j)
        variant = 0 if diagonal else jnp.where(c > qi, 2, 0)
        for mi in range(2):
            _qk_stage(k_refs[mi][0, 0, _key_rows(c, tq), :], qv_sc[variant + mi],
                      s_sc.at[2 * slot + mi], mb_sc.at[2 * slot + mi],
                      corr_sc[...] if diagonal else None)

    def sm(slot):
        return [_sm_stage(s_sc.at[2 * slot + mi], mb_sc.at[2 * slot + mi], p_sc.at[2 * slot + mi],
                          m_sc.at[mi]) for mi in range(2)]

    def pv(j, slot, alphas):
        c = chunk(j)
        vt = [vt_ref[0, 0, c * sub + n] for n in range(sub)]
        for mi in range(2):
            _pv_stage(vt, p_sc.at[2 * slot + mi], acc_sc.at[mi], alphas[mi])

    def step(j, slot):
        qk(j + 1, 1 - slot)
        alphas = sm(slot)
        pv(j - 1, 1 - slot, alphas)

    qk(0, 0, diagonal=True)
    qk(1, 1)
    sm(0)

    def pair(t, carry):
        step(2 * t + 1, 1)
        step(2 * t + 2, 0)
        return carry

    lax.fori_loop(0, nk // 2 - 1, pair, 0)
    alphas = sm(1)
    pv(nk - 2, 0, alphas)
    pv(nk - 1, 1, (None, None))

    lamp = lam_ref[...]
    linit = linit_ref[...]
    lam = (jnp.exp(jnp.sum(lamp[0:1] * lamp[1:2], axis=-1, keepdims=True))
           - jnp.exp(jnp.sum(lamp[2:3] * lamp[3:4], axis=-1, keepdims=True)) + linit)
    nv = 2 * DIFF_DIM
    a1, a2 = acc_sc[0], acc_sc[1]
    o = a1[:nv] / a1[nv:nv + 1] - lam * (a2[:nv] / a2[nv:nv + 1])
    o = _rms_norm(o.T, g_ref[...]) * (1.0 - linit)
    o_ref[0] = o.astype(o_ref.dtype)


def _diff_attention(layer, q1t, q2t, k1, k2, vt, lamp, linit, subln_g):
    b, hds, _, s = q1t.shape
    nc = vt.shape[2]
    t = DIFF_TILE
    assert t % KV_CHUNK == 0 and s % (2 * t) == 0
    f32, bf = jnp.float32, jnp.bfloat16
    qspec = pl.BlockSpec((1, 1, LANE, t), lambda bb, h, i: (bb, h, 0, i))
    kspec = pl.BlockSpec((1, 1, s, LANE), lambda bb, h, i: (bb, h, 0, 0))
    const = lambda shape: _layer_block(shape, layer)
    return pl.pallas_call(
        _diff_kernel,
        grid=(b, hds, s // t),
        in_specs=[
            qspec, qspec, kspec, kspec,
            pl.BlockSpec((1, 1, nc, V_ROWS, KV_CHUNK), lambda bb, h, i: (bb, h, 0, 0, 0)),
            const((4, DIFF_DIM)), const((1, 1)), const((1, 2 * DIFF_DIM)),
        ],
        out_specs=pl.BlockSpec((1, t, 2 * DIFF_DIM), lambda bb, h, i: (bb, i, h)),
        out_shape=jax.ShapeDtypeStruct((b, s, hds * 2 * DIFF_DIM), bf),
        scratch_shapes=[
            pltpu.VMEM((4, LANE, t), bf),
            pltpu.VMEM((t, t), f32),
            pltpu.VMEM((4, t, t), f32),
            pltpu.VMEM((4, t, t), bf),
            pltpu.VMEM((4, 1, t), f32),
            pltpu.VMEM((2, 1, t), f32),
            pltpu.VMEM((2, V_ROWS, t), f32),
        ],
        compiler_params=pltpu.CompilerParams(
            dimension_semantics=("parallel", "parallel", "arbitrary"), vmem_limit_bytes=VMEM_LIMIT),
        name="diff_attn",
    )(q1t, q2t, k1, k2, vt, lamp, linit, subln_g)


def _swap_halves(w):
    half = w.shape[-1] // 2
    return jnp.concatenate([w[..., half:], w[..., :half]], axis=-1)


def _prep_w_in(w_in):
    c_q = w_in[..., :256]
    c_kv = w_in[..., 256:384]
    k_rope = w_in[..., 384:448]
    rest = w_in[..., 448:]
    return jnp.concatenate([c_q, c_kv, k_rope, _swap_halves(k_rope), rest], axis=-1).astype(jnp.bfloat16)


def _prep_w_q_up(w_q_up):
    l, r, _ = w_q_up.shape
    w = w_q_up.reshape(l, r, MLA_HEADS, MLA_NOPE + MLA_ROPE)
    rope = w[..., MLA_NOPE:]
    w = jnp.concatenate([w[..., :MLA_NOPE], rope, _swap_halves(rope)], axis=-1)
    return w.reshape(l, r, MLA_HEADS * MLA_QK).astype(jnp.bfloat16)


def _rope_table(s):
    inv_freq = np.float32(ROPE_THETA) ** (-np.arange(0, MLA_ROPE, 2, dtype=np.float32) / np.float32(MLA_ROPE))
    ang = np.arange(s, dtype=np.float32)[:, None] * inv_freq[None, :]
    cos, sin = np.cos(ang), np.sin(ang)
    return jnp.asarray(np.concatenate([cos, cos, -sin, sin], axis=-1), jnp.float32)


def kernel(x, ffn1_w_gate, ffn1_w_up, ffn1_w_down, ln1_g, ln1_b, w_in, q_norm_g, w_q_up, kv_norm_g, w_kv_up, diff_lambda_q1, diff_lambda_k1, diff_lambda_q2, diff_lambda_k2, diff_subln_g, w_out, ln2_g, ln2_b, ffn2_w_gate, ffn2_w_up, ffn2_w_down, ln3_g, ln3_b):
    b, s, d = x.shape
    bf = jnp.bfloat16
    rope = _rope_table(s)
    w_in_p = _prep_w_in(w_in)
    w_q_p = _prep_w_q_up(w_q_up)
    w_kv_p = w_kv_up.astype(bf)
    w_out_p = w_out.astype(bf)
    f1 = (ffn1_w_gate.astype(bf), ffn1_w_up.astype(bf), ffn1_w_down.astype(bf))
    f2 = (ffn2_w_gate.astype(bf), ffn2_w_up.astype(bf), ffn2_w_down.astype(bf))
    lamp = jnp.stack([diff_lambda_q1, diff_lambda_k1, diff_lambda_q2, diff_lambda_k2], axis=1)
    linit = jnp.asarray([[[0.8 - 0.6 * math.exp(-0.3 * l)]] for l in range(DEPTH)], jnp.float32)
    rows = lambda v: v[:, None, :]

    xf = x.reshape(b * s, d)
    for l in range(DEPTH):
        xf = _ffn_ln(l, xf, *f1, rows(ln1_g), rows(ln1_b))
        qt, km, vtm, q1t, q2t, k1, k2, vtd = _inproj(
            l, xf.reshape(b, s, d), w_in_p, rows(q_norm_g), w_q_p, rows(kv_norm_g), w_kv_p, rope)
        o_mla = _mla_attention(qt, km, vtm)
        o_diff = _diff_attention(l, q1t, q2t, k1, k2, vtd, lamp, linit, rows(diff_subln_g))
        xf = _mix_ffn_ln(l, xf, o_mla.reshape(b * s, -1), o_diff.reshape(b * s, -1), w_out_p,
                         rows(ln2_g), rows(ln2_b), *f2, rows(ln3_g), rows(ln3_b))
    return xf.reshape(b, s, d)
```

```python
import math

import jax
import jax.numpy as jnp
import numpy as np
from jax import lax
from jax.experimental import pallas as pl
from jax.experimental.pallas import tpu as pltpu

D_MODEL = 1024
DEPTH = 4
MLA_HEADS = 4
MLA_NOPE = 128
MLA_ROPE = 64
MLA_V = 128
MLA_Q_RANK = 256
MLA_KV_RANK = 128
DIFF_HEADS = 4
DIFF_DIM = 64
D_FF = 2816
ROPE_THETA = 10000.0
NORM_EPS = 1e-5
ALPHA = (2 * DEPTH) ** 0.25

LANE = 128
ROW_TILE = 512
KV_CHUNK = ROW_TILE
DIFF_TILE = 1024
MLA_Q_TILE = 2048
MLA_K_TILE = 1024
POS_SPLIT = 128
VMEM_LIMIT = 58 * 1024 * 1024

MLA_QK = MLA_NOPE + 2 * MLA_ROPE
W_IN_COLS = 2048
NEG_INF = float("-inf")
LOG2E = math.log2(math.e)
AUG_ROWS = 16
V_ROWS = MLA_V + AUG_ROWS


def _bf16_pieces(x, n):
    out = []
    for _ in range(n):
        piece = float(np.asarray(x, dtype=jnp.bfloat16).astype(np.float32))
        out.append(piece)
        x -= piece
    return tuple(out)


LOG2E_PIECES = _bf16_pieces(LOG2E, 3)


def _dot(a, b):
    return jnp.dot(a, b, preferred_element_type=jnp.float32)


def _layer_norm(y, g, b):
    mu = jnp.mean(y, axis=-1, keepdims=True)
    d = y - mu
    var = jnp.mean(d * d, axis=-1, keepdims=True)
    return d * lax.rsqrt(var + NORM_EPS) * g + b


def _rms_norm(y, g):
    return y * lax.rsqrt(jnp.mean(y * y, axis=-1, keepdims=True) + NORM_EPS) * g


def _ffn_ln_rows(x, wg_ref, wu_ref, wd_ref, g_ref, b_ref):
    xb = x.astype(jnp.bfloat16)
    gate = _dot(xb, wg_ref[...])
    up = _dot(xb, wu_ref[...])
    h = gate / (1.0 + jnp.exp(-gate)) * up
    ffn = _dot(h.astype(jnp.bfloat16), wd_ref[...])
    return _layer_norm(ALPHA * x + 0.5 * ffn, g_ref[...], b_ref[...])


def _row_halves(n):
    return [slice(0, n // 2), slice(n // 2, n)]


def _ffn_ln_kernel(x_ref, wg_ref, wu_ref, wd_ref, g_ref, b_ref, o_ref):
    for rows in _row_halves(x_ref.shape[0]):
        o_ref[rows, :] = _ffn_ln_rows(x_ref[rows, :], wg_ref, wu_ref, wd_ref, g_ref, b_ref)


def _mix_ffn_ln_kernel(x_ref, om_ref, od_ref, wo_ref, g2_ref, b2_ref,
                       wg_ref, wu_ref, wd_ref, g3_ref, b3_ref, o_ref):
    nm = om_ref.shape[1]
    mix = _dot(om_ref[...], wo_ref[:nm, :]) + _dot(od_ref[...], wo_ref[nm:, :])
    x = _layer_norm(ALPHA * x_ref[...] + mix, g2_ref[...], b2_ref[...])
    o_ref[...] = _ffn_ln_rows(x, wg_ref, wu_ref, wd_ref, g3_ref, b3_ref)


def _row_tile(width):
    return pl.BlockSpec((ROW_TILE, width), lambda i: (i, 0))


def _layer_block(shape, layer):
    return pl.BlockSpec((None,) + tuple(shape), lambda *_: (layer,) + (0,) * len(shape),
                        pipeline_mode=pl.Buffered(1))


def _ffn_specs(layer):
    return [_layer_block((D_MODEL, D_FF), layer), _layer_block((D_MODEL, D_FF), layer),
            _layer_block((D_FF, D_MODEL), layer), _layer_block((1, D_MODEL), layer), _layer_block((1, D_MODEL), layer)]


def _ffn_ln(layer, x, wg, wu, wd, g, b):
    t = x.shape[0]
    return pl.pallas_call(
        _ffn_ln_kernel,
        grid=(t // ROW_TILE,),
        in_specs=[_row_tile(D_MODEL)] + _ffn_specs(layer),
        out_specs=_row_tile(D_MODEL),
        out_shape=jax.ShapeDtypeStruct((t, D_MODEL), jnp.float32),
        compiler_params=pltpu.CompilerParams(
            dimension_semantics=("parallel",), vmem_limit_bytes=VMEM_LIMIT),
        name="ffn_ln",
    )(x, wg, wu, wd, g, b)


def _mix_ffn_ln(layer, x, om, od, wo, g2, b2, wg, wu, wd, g3, b3):
    t = x.shape[0]
    nm, nd = om.shape[1], od.shape[1]
    return pl.pallas_call(
        _mix_ffn_ln_kernel,
        grid=(t // ROW_TILE,),
        in_specs=[_row_tile(D_MODEL), _row_tile(nm), _row_tile(nd), _layer_block((nm + nd, D_MODEL), layer),
                  _layer_block((1, D_MODEL), layer), _layer_block((1, D_MODEL), layer)] + _ffn_specs(layer),
        out_specs=_row_tile(D_MODEL),
        out_shape=jax.ShapeDtypeStruct((t, D_MODEL), jnp.float32),
        compiler_params=pltpu.CompilerParams(
            dimension_semantics=("parallel",), vmem_limit_bytes=VMEM_LIMIT),
        name="mix_ffn_ln",
    )(x, om, od, wo, g2, b2, wg, wu, wd, g3, b3)


def _alibi_slope(h):
    return 2.0 ** (-8.0 * (h + 1) / DIFF_HEADS)


def _inproj_kernel(x_ref, win_ref, qg_ref, wq_ref, kvg_ref, wkv_ref, rope_ref,
                   qt_ref, km_ref, vtm_ref, q1t_ref, q2t_ref, k1_ref, k2_ref, vtd_ref):
    i = pl.program_id(1)
    tm = x_ref.shape[1]
    xb = x_ref[0].astype(jnp.bfloat16)
    h = _dot(xb, win_ref[...])
    c_q = h[:, 0:256]
    c_kv = h[:, 256:384]
    kr = h[:, 384:512]
    dq = h[:, 512:1024]
    dk = h[:, 1024:1536]
    dv = h[:, 1536:2048]
    rope = rope_ref[...]

    q = _dot(_rms_norm(c_q, qg_ref[...]).astype(jnp.bfloat16), wq_ref[...])
    kv = _dot(_rms_norm(c_kv, kvg_ref[...]).astype(jnp.bfloat16), wkv_ref[...])
    krt = kr * rope
    k_rope = krt + pltpu.roll(krt, MLA_ROPE, axis=1)
    mla_scale = (MLA_NOPE + MLA_ROPE) ** -0.5 * LOG2E
    ones_rows = jnp.where(lax.broadcasted_iota(jnp.int32, (AUG_ROWS, tm), 0) == 0, 1.0, 0.0)
    for hh in range(MLA_HEADS):
        qh = q[:, hh * MLA_QK:(hh + 1) * MLA_QK]
        qh = jnp.concatenate([qh[:, :MLA_NOPE], qh[:, MLA_NOPE:] * rope], axis=1) * mla_scale
        qt_ref[0, hh] = qh.T.astype(jnp.bfloat16)
        kvh = kv[:, hh * 256:(hh + 1) * 256]
        km_ref[0, hh] = jnp.concatenate([kvh[:, :MLA_NOPE], k_rope], axis=1).astype(jnp.bfloat16)
        vtm_ref[0, hh, 0] = jnp.concatenate([kvh[:, MLA_NOPE:].T, ones_rows], axis=0).astype(jnp.bfloat16)

    pos_c = i * tm + lax.broadcasted_iota(jnp.int32, (AUG_ROWS, tm), 1)
    row = lax.broadcasted_iota(jnp.int32, (AUG_ROWS, tm), 0)
    hi_c = (pos_c & -POS_SPLIT).astype(jnp.float32)
    lo_c = (pos_c & (POS_SPLIT - 1)).astype(jnp.float32)
    pos_r = i * tm + lax.broadcasted_iota(jnp.int32, (tm, LANE), 0)
    lane = lax.broadcasted_iota(jnp.int32, (tm, LANE), 1)
    hi_r = (pos_r & -POS_SPLIT).astype(jnp.float32)
    lo_r = (pos_r & (POS_SPLIT - 1)).astype(jnp.float32)
    diff_scale = DIFF_DIM ** -0.5 * LOG2E
    zpad = jnp.zeros((DIFF_DIM - AUG_ROWS, tm), jnp.float32)
    def piece(slot):
        return jnp.where(slot < 4, LOG2E_PIECES[0], jnp.where(slot < 8, LOG2E_PIECES[1], LOG2E_PIECES[2]))

    slot_r = lane - DIFF_DIM
    piece_c, term_c = piece(row), row & 3
    piece_r, term_r = piece(slot_r), slot_r & 3
    for hh in range(DIFF_HEADS):
        slope = _alibi_slope(hh)
        qaug = jnp.where(row >= 12, 0.0,
                         jnp.where(term_c == 0, -slope * hi_c,
                                   jnp.where(term_c == 1, -slope * lo_c, piece_c)))
        kaug = jnp.where(slot_r >= 12, 0.0,
                         jnp.where(term_r < 2, piece_r,
                                   jnp.where(term_r == 2, slope * hi_r, slope * lo_r)))
        tq = (dq[:, hh * 128:(hh + 1) * 128] * diff_scale).T
        q1t_ref[0, hh] = jnp.concatenate([tq[:DIFF_DIM], qaug, zpad], axis=0).astype(jnp.bfloat16)
        q2t_ref[0, hh] = jnp.concatenate([tq[DIFF_DIM:], qaug, zpad], axis=0).astype(jnp.bfloat16)
        tk = dk[:, hh * 128:(hh + 1) * 128]
        k1_ref[0, hh] = jnp.where(lane < DIFF_DIM, tk, kaug).astype(jnp.bfloat16)
        k2_ref[0, hh] = jnp.where(lane < DIFF_DIM, pltpu.roll(tk, DIFF_DIM, axis=1), kaug).astype(jnp.bfloat16)
        vtd_ref[0, hh, 0] = jnp.concatenate(
            [dv[:, hh * 128:(hh + 1) * 128].T, ones_rows], axis=0).astype(jnp.bfloat16)


def _inproj(layer, x, win, qg, wq, kvg, wkv, rope):
    b, s, _ = x.shape
    nc = s // KV_CHUNK
    tm = ROW_TILE
    bf = jnp.bfloat16
    const = lambda shape: _layer_block(shape, layer)
    out_shape = (
        jax.ShapeDtypeStruct((b, MLA_HEADS, MLA_QK, s), bf),
        jax.ShapeDtypeStruct((b, MLA_HEADS, s, MLA_QK), bf),
        jax.ShapeDtypeStruct((b, MLA_HEADS, nc, V_ROWS, KV_CHUNK), bf),
        jax.ShapeDtypeStruct((b, DIFF_HEADS, LANE, s), bf),
        jax.ShapeDtypeStruct((b, DIFF_HEADS, LANE, s), bf),
        jax.ShapeDtypeStruct((b, DIFF_HEADS, s, LANE), bf),
        jax.ShapeDtypeStruct((b, DIFF_HEADS, s, LANE), bf),
        jax.ShapeDtypeStruct((b, DIFF_HEADS, nc, V_ROWS, KV_CHUNK), bf),
    )
    out_specs = (
        pl.BlockSpec((1, MLA_HEADS, MLA_QK, tm), lambda bb, i: (bb, 0, 0, i)),
        pl.BlockSpec((1, MLA_HEADS, tm, MLA_QK), lambda bb, i: (bb, 0, i, 0)),
        pl.BlockSpec((1, MLA_HEADS, 1, V_ROWS, tm), lambda bb, i: (bb, 0, i, 0, 0)),
        pl.BlockSpec((1, DIFF_HEADS, LANE, tm), lambda bb, i: (bb, 0, 0, i)),
        pl.BlockSpec((1, DIFF_HEADS, LANE, tm), lambda bb, i: (bb, 0, 0, i)),
        pl.BlockSpec((1, DIFF_HEADS, tm, LANE), lambda bb, i: (bb, 0, i, 0)),
        pl.BlockSpec((1, DIFF_HEADS, tm, LANE), lambda bb, i: (bb, 0, i, 0)),
        pl.BlockSpec((1, DIFF_HEADS, 1, V_ROWS, tm), lambda bb, i: (bb, 0, i, 0, 0)),
    )
    return pl.pallas_call(
        _inproj_kernel,
        grid=(b, s // tm),
        in_specs=[
            pl.BlockSpec((1, tm, D_MODEL), lambda bb, i: (bb, i, 0)),
            const((D_MODEL, W_IN_COLS)),
            const((1, MLA_Q_RANK)),
            const((MLA_Q_RANK, MLA_HEADS * MLA_QK)),
            const((1, MLA_KV_RANK)),
            const((MLA_KV_RANK, MLA_HEADS * 256)),
            pl.BlockSpec((tm, LANE), lambda bb, i: (i, 0)),
        ],
        out_specs=out_specs,
        out_shape=out_shape,
        compiler_params=pltpu.CompilerParams(
            dimension_semantics=("parallel", "parallel"), vmem_limit_bytes=VMEM_LIMIT),
        name="inproj",
    )(x, win, qg, wq, kvg, wkv, rope)


def _qk_stage(k_tile, q_tile, s_ref, mb_ref, corr=None):
    s = _dot(k_tile, q_tile)
    if corr is not None:
        s = s - corr
    s_ref[...] = s
    mb_ref[...] = jnp.max(s, axis=0, keepdims=True)


def _sm_stage(s_ref, mb_ref, p_ref, m_ref):
    m_old = m_ref[...]
    m_new = jnp.maximum(m_old, mb_ref[...])
    p_ref[...] = jnp.exp2(s_ref[...] - m_new).astype(p_ref.dtype)
    m_ref[...] = m_new
    return jnp.exp2(m_old - m_new)


def _pv_stage(vt_tiles, p_ref, acc_ref, alpha):
    acc = acc_ref[...]
    for n, vt_tile in enumerate(vt_tiles):
        acc = acc + _dot(vt_tile, p_ref[n * KV_CHUNK:(n + 1) * KV_CHUNK, :])
    acc_ref[...] = acc if alpha is None else alpha * acc


def _key_rows(c, tk):
    return pl.ds(pl.multiple_of(c * tk, tk), tk)


def _mla_kernel(qt_ref, k_ref, vt_ref, o_ref, s0, s1, p0, p1, mb0, mb1, m_sc, acc_sc):
    tk = s0.shape[0]
    sub = tk // KV_CHUNK
    nk = vt_ref.shape[2] // sub
    s_bufs, p_bufs, mb_bufs = (s0, s1), (p0, p1), (mb0, mb1)
    m_sc[...] = jnp.full_like(m_sc, NEG_INF)
    acc_sc[...] = jnp.zeros_like(acc_sc)

    tq = s0.shape[1]
    halves = [pl.ds(i * (tq // 2), tq // 2) for i in range(2)]

    def qk(c, slot):
        for hv in halves:
            _qk_stage(k_ref[0, 0, _key_rows(c, tk), :], qt_ref[0, 0, :, hv],
                      s_bufs[slot].at[:, hv], mb_bufs[slot].at[:, hv])

    def sm(slot):
        return [_sm_stage(s_bufs[slot].at[:, hv], mb_bufs[slot].at[:, hv], p_bufs[slot].at[:, hv],
                          m_sc.at[:, hv]) for hv in halves]

    def pv(c, slot, alphas):
        vt = [vt_ref[0, 0, c * sub + n] for n in range(sub)]
        for i, hv in enumerate(halves):
            _pv_stage(vt, p_bufs[slot].at[:, hv], acc_sc.at[:, hv], None if alphas is None else alphas[i])

    def step(c, slot):
        qk(c + 1, 1 - slot)
        alpha = sm(slot)
        pv(c - 1, 1 - slot, alpha)

    qk(0, 0)
    qk(1, 1)
    sm(0)

    def pair(t, carry):
        step(2 * t + 1, 1)
        step(2 * t + 2, 0)
        return carry

    lax.fori_loop(0, nk // 2 - 1, pair, 0)
    alpha = sm(1)
    pv(nk - 2, 0, alpha)
    pv(nk - 1, 1, None)
    acc = acc_sc[...]
    o = acc[:MLA_V] / acc[MLA_V:MLA_V + 1]
    o_ref[0] = o.T.astype(o_ref.dtype)


def _mla_attention(qt, k, vt):
    b, hds, _, s = qt.shape
    nc = vt.shape[2]
    tq, tk = MLA_Q_TILE, MLA_K_TILE
    assert tk % KV_CHUNK == 0 and s % (2 * tk) == 0
    f32, bf = jnp.float32, jnp.bfloat16
    return pl.pallas_call(
        _mla_kernel,
        grid=(b, hds, s // tq),
        in_specs=[
            pl.BlockSpec((1, 1, MLA_QK, tq), lambda bb, h, i: (bb, h, 0, i)),
            pl.BlockSpec((1, 1, s, MLA_QK), lambda bb, h, i: (bb, h, 0, 0)),
            pl.BlockSpec((1, 1, nc, V_ROWS, KV_CHUNK), lambda bb, h, i: (bb, h, 0, 0, 0)),
        ],
        out_specs=pl.BlockSpec((1, tq, MLA_V), lambda bb, h, i: (bb, i, h)),
        out_shape=jax.ShapeDtypeStruct((b, s, hds * MLA_V), bf),
        scratch_shapes=[
            pltpu.VMEM((tk, tq), f32), pltpu.VMEM((tk, tq), f32),
            pltpu.VMEM((tk, tq), bf), pltpu.VMEM((tk, tq), bf),
            pltpu.VMEM((1, tq), f32), pltpu.VMEM((1, tq), f32),
            pltpu.VMEM((1, tq), f32),
            pltpu.VMEM((V_ROWS, tq), f32),
        ],
        compiler_params=pltpu.CompilerParams(
            dimension_semantics=("parallel", "parallel", "arbitrary"), vmem_limit_bytes=VMEM_LIMIT),
        name="mla_attn",
    )(qt, k, vt)


def _diff_kernel(q1t_ref, q2t_ref, k1_ref, k2_ref, vt_ref, lam_ref, linit_ref, g_ref, o_ref,
                 qv_sc, corr_sc, s_sc, p_sc, mb_sc, m_sc, acc_sc):
    h = pl.program_id(1)
    qi = pl.program_id(2)
    tq = q1t_ref.shape[3]
    sub = tq // KV_CHUNK
    nk = vt_ref.shape[2] // sub
    k_refs = (k1_ref, k2_ref)
    m_sc[...] = jnp.full_like(m_sc, NEG_INF)
    acc_sc[...] = jnp.zeros_like(acc_sc)
    row = lax.broadcasted_iota(jnp.int32, (LANE, tq), 0)
    for mi, qref in enumerate((q1t_ref, q2t_ref)):
        qq = qref[0, 0]
        qv_sc[mi] = qq
        qv_sc[2 + mi] = jnp.where(row < DIFF_DIM, qq, -qq)

    @pl.when(qi == 0)
    def _():
        jj = lax.broadcasted_iota(jnp.int32, (tq, tq), 0)
        ii = lax.broadcasted_iota(jnp.int32, (tq, tq), 1)
        expo = jnp.full((1, tq), 127 + 1, jnp.int32) - 2 * (h + 1)
        slope2 = lax.bitcast_convert_type(expo << 23, jnp.float32)
        corr_sc[...] = (slope2 * LOG2E) * jnp.maximum(jj - ii, 0).astype(jnp.float32)

    def chunk(j):
        return lax.rem(qi + j, nk)

    def qk(j, slot, diagonal=False):
        c = chunk(j)
        variant = 0 if diagonal else jnp.where(c > qi, 2, 0)
        for mi in range(2):
            _qk_stage(k_refs[mi][0, 0, _key_rows(c, tq), :], qv_sc[variant + mi],
                      s_sc.at[2 * slot + mi], mb_sc.at[2 * slot + mi],
                      corr_sc[...] if diagonal else None)

    def sm(slot):
        return [_sm_stage(s_sc.at[2 * slot + mi], mb_sc.at[2 * slot + mi], p_sc.at[2 * slot + mi],
                          m_sc.at[mi]) for mi in range(2)]

    def pv(j, slot, alphas):
        c = chunk(j)
        vt = [vt_ref[0, 0, c * sub + n] for n in range(sub)]
        for mi in range(2):
            _pv_stage(vt, p_sc.at[2 * slot + mi], acc_sc.at[mi], alphas[mi])

    def step(j, slot):
        qk(j + 1, 1 - slot)
        alphas = sm(slot)
        pv(j - 1, 1 - slot, alphas)

    qk(0, 0, diagonal=True)
    qk(1, 1)
    sm(0)

    def pair(t, carry):
        step(2 * t + 1, 1)
        step(2 * t + 2, 0)
        return carry

    lax.fori_loop(0, nk // 2 - 1, pair, 0)
    alphas = sm(1)
    pv(nk - 2, 0, alphas)
    pv(nk - 1, 1, (None, None))

    lamp = lam_ref[...]
    linit = linit_ref[...]
    lam = (jnp.exp(jnp.sum(lamp[0:1] * lamp[1:2], axis=-1, keepdims=True))
           - jnp.exp(jnp.sum(lamp[2:3] * lamp[3:4], axis=-1, keepdims=True)) + linit)
    nv = 2 * DIFF_DIM
    a1, a2 = acc_sc[0], acc_sc[1]
    o = a1[:nv] / a1[nv:nv + 1] - lam * (a2[:nv] / a2[nv:nv + 1])
    o = _rms_norm(o.T, g_ref[...]) * (1.0 - linit)
    o_ref[0] = o.astype(o_ref.dtype)


def _diff_attention(layer, q1t, q2t, k1, k2, vt, lamp, linit, subln_g):
    b, hds, _, s = q1t.shape
    nc = vt.shape[2]
    t = DIFF_TILE
    assert t % KV_CHUNK == 0 and s % (2 * t) == 0
    f32, bf = jnp.float32, jnp.bfloat16
    qspec = pl.BlockSpec((1, 1, LANE, t), lambda bb, h, i: (bb, h, 0, i))
    kspec = pl.BlockSpec((1, 1, s, LANE), lambda bb, h, i: (bb, h, 0, 0))
    const = lambda shape: _layer_block(shape, layer)
    return pl.pallas_call(
        _diff_kernel,
        grid=(b, hds, s // t),
        in_specs=[
            qspec, qspec, kspec, kspec,
            pl.BlockSpec((1, 1, nc, V_ROWS, KV_CHUNK), lambda bb, h, i: (bb, h, 0, 0, 0)),
            const((4, DIFF_DIM)), const((1, 1)), const((1, 2 * DIFF_DIM)),
        ],
        out_specs=pl.BlockSpec((1, t, 2 * DIFF_DIM), lambda bb, h, i: (bb, i, h)),
        out_shape=jax.ShapeDtypeStruct((b, s, hds * 2 * DIFF_DIM), bf),
        scratch_shapes=[
            pltpu.VMEM((4, LANE, t), bf),
            pltpu.VMEM((t, t), f32),
            pltpu.VMEM((4, t, t), f32),
            pltpu.VMEM((4, t, t), bf),
            pltpu.VMEM((4, 1, t), f32),
            pltpu.VMEM((2, 1, t), f32),
            pltpu.VMEM((2, V_ROWS, t), f32),
        ],
        compiler_params=pltpu.CompilerParams(
            dimension_semantics=("parallel", "parallel", "arbitrary"), vmem_limit_bytes=VMEM_LIMIT),
        name="diff_attn",
    )(q1t, q2t, k1, k2, vt, lamp, linit, subln_g)


def _swap_halves(w):
    half = w.shape[-1] // 2
    return jnp.concatenate([w[..., half:], w[..., :half]], axis=-1)


def _prep_w_in(w_in):
    w = w_in.astype(jnp.bfloat16)
    k_rope = w[..., 384:448]
    return jnp.concatenate([w[..., :448], _swap_halves(k_rope), w[..., 448:]], axis=-1)


def _prep_w_q_up(w_q_up):
    l, r, _ = w_q_up.shape
    w = w_q_up.astype(jnp.bfloat16).reshape(l, r, MLA_HEADS, MLA_NOPE + MLA_ROPE)
    w = jnp.concatenate([w, _swap_halves(w[..., MLA_NOPE:])], axis=-1)
    return w.reshape(l, r, MLA_HEADS * MLA_QK)


def _rope_table(s):
    inv_freq = np.float32(ROPE_THETA) ** (-np.arange(0, MLA_ROPE, 2, dtype=np.float32) / np.float32(MLA_ROPE))
    ang = np.arange(s, dtype=np.float32)[:, None] * inv_freq[None, :]
    cos, sin = np.cos(ang), np.sin(ang)
    return jnp.asarray(np.concatenate([cos, cos, -sin, sin], axis=-1), jnp.float32)


def kernel(x, ffn1_w_gate, ffn1_w_up, ffn1_w_down, ln1_g, ln1_b, w_in, q_norm_g, w_q_up, kv_norm_g, w_kv_up, diff_lambda_q1, diff_lambda_k1, diff_lambda_q2, diff_lambda_k2, diff_subln_g, w_out, ln2_g, ln2_b, ffn2_w_gate, ffn2_w_up, ffn2_w_down, ln3_g, ln3_b):
    b, s, d = x.shape
    bf = jnp.bfloat16
    rope = _rope_table(s)
    w_in_p = _prep_w_in(w_in)
    w_q_p = _prep_w_q_up(w_q_up)
    w_kv_p = w_kv_up.astype(bf)
    w_out_p = w_out.astype(bf)
    f1 = (ffn1_w_gate.astype(bf), ffn1_w_up.astype(bf), ffn1_w_down.astype(bf))
    f2 = (ffn2_w_gate.astype(bf), ffn2_w_up.astype(bf), ffn2_w_down.astype(bf))
    lamp = jnp.stack([diff_lambda_q1, diff_lambda_k1, diff_lambda_q2, diff_lambda_k2], axis=1)
    linit = jnp.asarray([[[0.8 - 0.6 * math.exp(-0.3 * l)]] for l in range(DEPTH)], jnp.float32)
    rows = lambda v: v[:, None, :]

    xf = x.reshape(b * s, d)
    for l in range(DEPTH):
        xf = _ffn_ln(l, xf, *f1, rows(ln1_g), rows(ln1_b))
        qt, km, vtm, q1t, q2t, k1, k2, vtd = _inproj(
            l, xf.reshape(b, s, d), w_in_p, rows(q_norm_g), w_q_p, rows(kv_norm_g), w_kv_p, rope)
        o_mla = _mla_attention(qt, km, vtm)
        o_diff = _diff_attention(l, q1t, q2t, k1, k2, vtd, lamp, linit, rows(diff_subln_g))
        xf = _mix_ffn_ln(l, xf, o_mla.reshape(b * s, -1), o_diff.reshape(b * s, -1), w_out_p,
                         rows(ln2_g), rows(ln2_b), *f2, rows(ln3_g), rows(ln3_b))
    return xf.reshape(b, s, d)
```

```python
import math

import jax
import jax.numpy as jnp
import numpy as np
from jax import lax
from jax.experimental import pallas as pl
from jax.experimental.pallas import tpu as pltpu

D_MODEL = 1024
DEPTH = 4
MLA_HEADS = 4
MLA_NOPE = 128
MLA_ROPE = 64
MLA_V = 128
MLA_Q_RANK = 256
MLA_KV_RANK = 128
DIFF_HEADS = 4
DIFF_DIM = 64
D_FF = 2816
ROPE_THETA = 10000.0
NORM_EPS = 1e-5
ALPHA = (2 * DEPTH) ** 0.25

LANE = 128
ROW_TILE = 512
KV_CHUNK = ROW_TILE
DIFF_TILE = 1024
MLA_Q_TILE = 2048
MLA_K_TILE = 1024
POS_SPLIT = 128
VMEM_V7X = 64 * 1024 * 1024
VMEM_LIMIT = VMEM_V7X - 6 * 1024 * 1024

MLA_QK = MLA_NOPE + 2 * MLA_ROPE
MLA_KV = MLA_NOPE + MLA_V
DIFF_W = DIFF_HEADS * 2 * DIFF_DIM
W_IN_SPLITS = (MLA_Q_RANK, MLA_KV_RANK, 2 * MLA_ROPE, DIFF_W, DIFF_W, DIFF_W)
W_IN_COLS = sum(W_IN_SPLITS)
NEG_INF = float("-inf")
LOG2E = math.log2(math.e)
AUG_ROWS = 16
V_ROWS = MLA_V + AUG_ROWS


def _bf16_pieces(x, n):
    out = []
    for _ in range(n):
        piece = float(np.asarray(x, dtype=jnp.bfloat16).astype(np.float32))
        out.append(piece)
        x -= piece
    return tuple(out)


LOG2E_PIECES = _bf16_pieces(LOG2E, 3)


def _dot(a, b):
    return jnp.dot(a, b, preferred_element_type=jnp.float32)


def _layer_norm(y, g, b):
    mu = jnp.mean(y, axis=-1, keepdims=True)
    d = y - mu
    var = jnp.mean(d * d, axis=-1, keepdims=True)
    return d * lax.rsqrt(var + NORM_EPS) * g + b


def _rms_norm(y, g):
    return y * lax.rsqrt(jnp.mean(y * y, axis=-1, keepdims=True) + NORM_EPS) * g


def _ffn_ln_rows(x, wg_ref, wu_ref, wd_ref, g_ref, b_ref):
    xb = x.astype(jnp.bfloat16)
    gate = _dot(xb, wg_ref[...])
    up = _dot(xb, wu_ref[...])
    h = gate / (1.0 + jnp.exp(-gate)) * up
    ffn = _dot(h.astype(jnp.bfloat16), wd_ref[...])
    return _layer_norm(ALPHA * x + 0.5 * ffn, g_ref[...], b_ref[...])


def _row_halves(n):
    return [slice(0, n // 2), slice(n // 2, n)]


def _ffn_ln_kernel(x_ref, wg_ref, wu_ref, wd_ref, g_ref, b_ref, o_ref):
    for rows in _row_halves(x_ref.shape[0]):
        o_ref[rows, :] = _ffn_ln_rows(x_ref[rows, :], wg_ref, wu_ref, wd_ref, g_ref, b_ref)


def _mix_ffn_ln_kernel(x_ref, om_ref, od_ref, wo_ref, g2_ref, b2_ref,
                       wg_ref, wu_ref, wd_ref, g3_ref, b3_ref, o_ref):
    nm = om_ref.shape[1]
    mix = _dot(om_ref[...], wo_ref[:nm, :]) + _dot(od_ref[...], wo_ref[nm:, :])
    x = _layer_norm(ALPHA * x_ref[...] + mix, g2_ref[...], b2_ref[...])
    o_ref[...] = _ffn_ln_rows(x, wg_ref, wu_ref, wd_ref, g3_ref, b3_ref)


def _row_tile(width):
    return pl.BlockSpec((ROW_TILE, width), lambda i: (i, 0))


def _layer_block(shape, layer):
    return pl.BlockSpec((None,) + tuple(shape), lambda *_: (layer,) + (0,) * len(shape),
                        pipeline_mode=pl.Buffered(1))


def _ffn_specs(layer):
    return [_layer_block((D_MODEL, D_FF), layer), _layer_block((D_MODEL, D_FF), layer),
            _layer_block((D_FF, D_MODEL), layer), _layer_block((1, D_MODEL), layer), _layer_block((1, D_MODEL), layer)]


def _ffn_ln(layer, x, wg, wu, wd, g, b):
    t = x.shape[0]
    return pl.pallas_call(
        _ffn_ln_kernel,
        grid=(t // ROW_TILE,),
        in_specs=[_row_tile(D_MODEL)] + _ffn_specs(layer),
        out_specs=_row_tile(D_MODEL),
        out_shape=jax.ShapeDtypeStruct((t, D_MODEL), jnp.float32),
        compiler_params=pltpu.CompilerParams(
            dimension_semantics=("parallel",), vmem_limit_bytes=VMEM_LIMIT),
        name="ffn_ln",
    )(x, wg, wu, wd, g, b)


def _mix_ffn_ln(layer, x, om, od, wo, g2, b2, wg, wu, wd, g3, b3):
    t = x.shape[0]
    nm, nd = om.shape[1], od.shape[1]
    return pl.pallas_call(
        _mix_ffn_ln_kernel,
        grid=(t // ROW_TILE,),
        in_specs=[_row_tile(D_MODEL), _row_tile(nm), _row_tile(nd), _layer_block((nm + nd, D_MODEL), layer),
                  _layer_block((1, D_MODEL), layer), _layer_block((1, D_MODEL), layer)] + _ffn_specs(layer),
        out_specs=_row_tile(D_MODEL),
        out_shape=jax.ShapeDtypeStruct((t, D_MODEL), jnp.float32),
        compiler_params=pltpu.CompilerParams(
            dimension_semantics=("parallel",), vmem_limit_bytes=VMEM_LIMIT),
        name="mix_ffn_ln",
    )(x, om, od, wo, g2, b2, wg, wu, wd, g3, b3)


def _alibi_slope(h):
    return 2.0 ** (-8.0 * (h + 1) / DIFF_HEADS)


def _inproj_kernel(x_ref, win_ref, qg_ref, wq_ref, kvg_ref, wkv_ref, rope_ref,
                   qt_ref, km_ref, vtm_ref, q1t_ref, q2t_ref, k1_ref, k2_ref, vtd_ref):
    i = pl.program_id(1)
    tm = x_ref.shape[1]
    xb = x_ref[0].astype(jnp.bfloat16)
    h = _dot(xb, win_ref[...])
    edges = np.cumsum((0,) + W_IN_SPLITS)
    c_q, c_kv, kr, dq, dk, dv = (h[:, int(a):int(b)] for a, b in zip(edges[:-1], edges[1:]))
    rope = rope_ref[...]

    q = _dot(_rms_norm(c_q, qg_ref[...]).astype(jnp.bfloat16), wq_ref[...])
    kv = _dot(_rms_norm(c_kv, kvg_ref[...]).astype(jnp.bfloat16), wkv_ref[...])
    krt = kr * rope
    k_rope = krt + pltpu.roll(krt, MLA_ROPE, axis=1)
    mla_scale = (MLA_NOPE + MLA_ROPE) ** -0.5 * LOG2E
    ones_rows = jnp.where(lax.broadcasted_iota(jnp.int32, (AUG_ROWS, tm), 0) == 0, 1.0, 0.0)
    for hh in range(MLA_HEADS):
        qh = q[:, hh * MLA_QK:(hh + 1) * MLA_QK]
        qh = jnp.concatenate([qh[:, :MLA_NOPE], qh[:, MLA_NOPE:] * rope], axis=1) * mla_scale
        qt_ref[0, hh] = qh.T.astype(jnp.bfloat16)
        kvh = kv[:, hh * MLA_KV:(hh + 1) * MLA_KV]
        km_ref[0, hh] = jnp.concatenate([kvh[:, :MLA_NOPE], k_rope], axis=1).astype(jnp.bfloat16)
        vtm_ref[0, hh, 0] = jnp.concatenate([kvh[:, MLA_NOPE:].T, ones_rows], axis=0).astype(jnp.bfloat16)

    pos_c = i * tm + lax.broadcasted_iota(jnp.int32, (AUG_ROWS, tm), 1)
    row = lax.broadcasted_iota(jnp.int32, (AUG_ROWS, tm), 0)
    hi_c = (pos_c & -POS_SPLIT).astype(jnp.float32)
    lo_c = (pos_c & (POS_SPLIT - 1)).astype(jnp.float32)
    pos_r = i * tm + lax.broadcasted_iota(jnp.int32, (tm, LANE), 0)
    lane = lax.broadcasted_iota(jnp.int32, (tm, LANE), 1)
    hi_r = (pos_r & -POS_SPLIT).astype(jnp.float32)
    lo_r = (pos_r & (POS_SPLIT - 1)).astype(jnp.float32)
    diff_scale = DIFF_DIM ** -0.5 * LOG2E
    zpad = jnp.zeros((DIFF_DIM - AUG_ROWS, tm), jnp.float32)
    def piece(slot):
        return jnp.where(slot < 4, LOG2E_PIECES[0], jnp.where(slot < 8, LOG2E_PIECES[1], LOG2E_PIECES[2]))

    slot_r = lane - DIFF_DIM
    piece_c, term_c = piece(row), row & 3
    piece_r, term_r = piece(slot_r), slot_r & 3
    for hh in range(DIFF_HEADS):
        slope = _alibi_slope(hh)
        qaug = jnp.where(row >= 12, 0.0,
                         jnp.where(term_c == 0, -slope * hi_c,
                                   jnp.where(term_c == 1, -slope * lo_c, piece_c)))
        kaug = jnp.where(slot_r >= 12, 0.0,
                         jnp.where(term_r < 2, piece_r,
                                   jnp.where(term_r == 2, slope * hi_r, slope * lo_r)))
        per_head = slice(hh * 2 * DIFF_DIM, (hh + 1) * 2 * DIFF_DIM)
        tq = (dq[:, per_head] * diff_scale).T
        q1t_ref[0, hh] = jnp.concatenate([tq[:DIFF_DIM], qaug, zpad], axis=0).astype(jnp.bfloat16)
        q2t_ref[0, hh] = jnp.concatenate([tq[DIFF_DIM:], qaug, zpad], axis=0).astype(jnp.bfloat16)
        tk = dk[:, per_head]
        k1_ref[0, hh] = jnp.where(lane < DIFF_DIM, tk, kaug).astype(jnp.bfloat16)
        k2_ref[0, hh] = jnp.where(lane < DIFF_DIM, pltpu.roll(tk, DIFF_DIM, axis=1), kaug).astype(jnp.bfloat16)
        vtd_ref[0, hh, 0] = jnp.concatenate(
            [dv[:, per_head].T, ones_rows], axis=0).astype(jnp.bfloat16)


def _inproj(layer, x, win, qg, wq, kvg, wkv, rope):
    b, s, _ = x.shape
    nc = s // KV_CHUNK
    tm = ROW_TILE
    bf = jnp.bfloat16
    const = lambda shape: _layer_block(shape, layer)
    out_shape = (
        jax.ShapeDtypeStruct((b, MLA_HEADS, MLA_QK, s), bf),
        jax.ShapeDtypeStruct((b, MLA_HEADS, s, MLA_QK), bf),
        jax.ShapeDtypeStruct((b, MLA_HEADS, nc, V_ROWS, KV_CHUNK), bf),
        jax.ShapeDtypeStruct((b, DIFF_HEADS, LANE, s), bf),
        jax.ShapeDtypeStruct((b, DIFF_HEADS, LANE, s), bf),
        jax.ShapeDtypeStruct((b, DIFF_HEADS, s, LANE), bf),
        jax.ShapeDtypeStruct((b, DIFF_HEADS, s, LANE), bf),
        jax.ShapeDtypeStruct((b, DIFF_HEADS, nc, V_ROWS, KV_CHUNK), bf),
    )
    out_specs = (
        pl.BlockSpec((1, MLA_HEADS, MLA_QK, tm), lambda bb, i: (bb, 0, 0, i)),
        pl.BlockSpec((1, MLA_HEADS, tm, MLA_QK), lambda bb, i: (bb, 0, i, 0)),
        pl.BlockSpec((1, MLA_HEADS, 1, V_ROWS, tm), lambda bb, i: (bb, 0, i, 0, 0)),
        pl.BlockSpec((1, DIFF_HEADS, LANE, tm), lambda bb, i: (bb, 0, 0, i)),
        pl.BlockSpec((1, DIFF_HEADS, LANE, tm), lambda bb, i: (bb, 0, 0, i)),
        pl.BlockSpec((1, DIFF_HEADS, tm, LANE), lambda bb, i: (bb, 0, i, 0)),
        pl.BlockSpec((1, DIFF_HEADS, tm, LANE), lambda bb, i: (bb, 0, i, 0)),
        pl.BlockSpec((1, DIFF_HEADS, 1, V_ROWS, tm), lambda bb, i: (bb, 0, i, 0, 0)),
    )
    return pl.pallas_call(
        _inproj_kernel,
        grid=(b, s // tm),
        in_specs=[
            pl.BlockSpec((1, tm, D_MODEL), lambda bb, i: (bb, i, 0)),
            const((D_MODEL, W_IN_COLS)),
            const((1, MLA_Q_RANK)),
            const((MLA_Q_RANK, MLA_HEADS * MLA_QK)),
            const((1, MLA_KV_RANK)),
            const((MLA_KV_RANK, MLA_HEADS * MLA_KV)),
            pl.BlockSpec((tm, LANE), lambda bb, i: (i, 0)),
        ],
        out_specs=out_specs,
        out_shape=out_shape,
        compiler_params=pltpu.CompilerParams(
            dimension_semantics=("parallel", "parallel"), vmem_limit_bytes=VMEM_LIMIT),
        name="inproj",
    )(x, win, qg, wq, kvg, wkv, rope)


def _qk_stage(k_tile, q_tile, s_ref, mb_ref, corr=None):
    s = _dot(k_tile, q_tile)
    if corr is not None:
        s = s - corr
    s_ref[...] = s
    mb_ref[...] = jnp.max(s, axis=0, keepdims=True)


def _sm_stage(s_ref, mb_ref, p_ref, m_ref):
    m_old = m_ref[...]
    m_new = jnp.maximum(m_old, mb_ref[...])
    p_ref[...] = jnp.exp2(s_ref[...] - m_new).astype(p_ref.dtype)
    m_ref[...] = m_new
    return jnp.exp2(m_old - m_new)


def _pv_stage(vt_tiles, p_ref, acc_ref, alpha):
    acc = acc_ref[...]
    for n, vt_tile in enumerate(vt_tiles):
        acc = acc + _dot(vt_tile, p_ref[n * KV_CHUNK:(n + 1) * KV_CHUNK, :])
    if alpha is None:
        return acc
    acc_ref[...] = alpha * acc


def _key_rows(c, tk):
    return pl.ds(pl.multiple_of(c * tk, tk), tk)


def _query_cols(tile, tq):
    return pl.ds(pl.multiple_of(tile * tq, tq), tq)


def _tile_pipeline(qi, nq, nk, *, setup, qk, sm, pv, restart, finish):
    @pl.when(qi == 0)
    def _():
        setup()
        restart()
        qk(0, 0, 0)
        qk(0, 1, 1)
        sm(0)

    def step(j, slot):
        qk(qi, j + 1, 1 - slot)
        pv(j - 1, 1 - slot, sm(slot))

    def pair(t, carry):
        step(2 * t + 1, 1)
        step(2 * t + 2, 0)
        return carry

    lax.fori_loop(0, nk // 2 - 1, pair, 0)

    @pl.when(qi < nq - 1)
    def _():
        qk(qi + 1, 0, 0)
        pv(nk - 2, 0, sm(1))
        qk(qi + 1, 1, 1)
        restart()
        sm(0)
        finish(pv(nk - 1, 1, None))

    @pl.when(qi == nq - 1)
    def _():
        pv(nk - 2, 0, sm(1))
        finish(pv(nk - 1, 1, None))


def _mla_kernel(qt_ref, k_ref, vt_ref, o_ref, s0, s1, p0, p1, mb0, mb1, m_sc, acc_sc):
    qi = pl.program_id(2)
    tk, tq = s0.shape
    sub = tk // KV_CHUNK
    nk = vt_ref.shape[2] // sub
    s_bufs, p_bufs, mb_bufs = (s0, s1), (p0, p1), (mb0, mb1)
    halves = [pl.ds(i * (tq // 2), tq // 2) for i in range(2)]

    def setup():
        acc_sc[...] = jnp.zeros_like(acc_sc)

    def qk(tile, j, slot):
        for i, hv in enumerate(halves):
            cols = pl.ds(pl.multiple_of(tile * tq + i * (tq // 2), tq // 2), tq // 2)
            _qk_stage(k_ref[0, 0, _key_rows(j, tk), :], qt_ref[0, 0, :, cols],
                      s_bufs[slot].at[:, hv], mb_bufs[slot].at[:, hv])

    def sm(slot):
        return [_sm_stage(s_bufs[slot].at[:, hv], mb_bufs[slot].at[:, hv], p_bufs[slot].at[:, hv],
                          m_sc.at[:, hv]) for hv in halves]

    def pv(j, slot, alphas):
        vt = [vt_ref[0, 0, j * sub + n] for n in range(sub)]
        return [_pv_stage(vt, p_bufs[slot].at[:, hv], acc_sc.at[:, hv], None if alphas is None else alphas[i])
                for i, hv in enumerate(halves)]

    def restart():
        m_sc[...] = jnp.full_like(m_sc, NEG_INF)

    def finish(totals):
        acc = jnp.concatenate(totals, axis=1)
        o = acc[:MLA_V] / acc[MLA_V:MLA_V + 1]
        o_ref[0] = o.T.astype(o_ref.dtype)
        acc_sc[...] = jnp.zeros_like(acc_sc)

    _tile_pipeline(qi, qt_ref.shape[3] // tq, nk, setup=setup, qk=qk, sm=sm, pv=pv, restart=restart, finish=finish)


def _mla_attention(qt, k, vt):
    b, hds, _, s = qt.shape
    nc = vt.shape[2]
    tq, tk = MLA_Q_TILE, MLA_K_TILE
    assert tk % KV_CHUNK == 0 and s % (2 * tk) == 0 and s % tq == 0
    f32, bf = jnp.float32, jnp.bfloat16
    return pl.pallas_call(
        _mla_kernel,
        grid=(b, hds, s // tq),
        in_specs=[
            pl.BlockSpec((1, 1, MLA_QK, s), lambda bb, h, i: (bb, h, 0, 0)),
            pl.BlockSpec((1, 1, s, MLA_QK), lambda bb, h, i: (bb, h, 0, 0)),
            pl.BlockSpec((1, 1, nc, V_ROWS, KV_CHUNK), lambda bb, h, i: (bb, h, 0, 0, 0)),
        ],
        out_specs=pl.BlockSpec((1, tq, MLA_V), lambda bb, h, i: (bb, i, h)),
        out_shape=jax.ShapeDtypeStruct((b, s, hds * MLA_V), bf),
        scratch_shapes=[
            pltpu.VMEM((tk, tq), f32), pltpu.VMEM((tk, tq), f32),
            pltpu.VMEM((tk, tq), bf), pltpu.VMEM((tk, tq), bf),
            pltpu.VMEM((1, tq), f32), pltpu.VMEM((1, tq), f32),
            pltpu.VMEM((1, tq), f32),
            pltpu.VMEM((V_ROWS, tq), f32),
        ],
        compiler_params=pltpu.CompilerParams(
            dimension_semantics=("parallel", "parallel", "arbitrary"), vmem_limit_bytes=VMEM_LIMIT),
        name="mla_attn",
    )(qt, k, vt)


def _diff_kernel(q1t_ref, q2t_ref, k1_ref, k2_ref, vt_ref, lam_ref, linit_ref, g_ref, o_ref,
                 qv_sc, corr_sc, s_sc, p_sc, mb_sc, m_sc, acc_sc):
    h = pl.program_id(1)
    qi = pl.program_id(2)
    tq = o_ref.shape[1]
    sub = tq // KV_CHUNK
    nk = vt_ref.shape[2] // sub
    k_refs = (k1_ref, k2_ref)
    q_refs = (q1t_ref, q2t_ref)

    def setup():
        jj = lax.broadcasted_iota(jnp.int32, (tq, tq), 0)
        ii = lax.broadcasted_iota(jnp.int32, (tq, tq), 1)
        expo = jnp.full((1, tq), 127 + 1, jnp.int32) - 2 * (h + 1)
        slope2 = lax.bitcast_convert_type(expo << 23, jnp.float32)
        corr_sc[...] = (slope2 * LOG2E) * jnp.maximum(jj - ii, 0).astype(jnp.float32)
        acc_sc[...] = jnp.zeros_like(acc_sc)

    def chunk(tile, j):
        return lax.rem(tile + j, nk)

    def load_queries(tile):
        row = lax.broadcasted_iota(jnp.int32, (LANE, tq), 0)
        for mi in range(2):
            qq = q_refs[mi][0, 0, :, _query_cols(tile, tq)]
            qv_sc[mi] = qq
            qv_sc[2 + mi] = jnp.where(row < DIFF_DIM, qq, -qq)

    def qk(tile, j, slot):
        diagonal = isinstance(j, int) and j == 0
        if diagonal:
            load_queries(tile)
        c = chunk(tile, j)
        variant = 0 if diagonal else jnp.where(c > tile, 2, 0)
        for mi in range(2):
            _qk_stage(k_refs[mi][0, 0, _key_rows(c, tq), :], qv_sc[variant + mi],
                      s_sc.at[2 * slot + mi], mb_sc.at[2 * slot + mi],
                      corr_sc[...] if diagonal else None)

    def sm(slot):
        return [_sm_stage(s_sc.at[2 * slot + mi], mb_sc.at[2 * slot + mi], p_sc.at[2 * slot + mi],
                          m_sc.at[mi]) for mi in range(2)]

    def pv(j, slot, alphas):
        c = chunk(qi, j)
        vt = [vt_ref[0, 0, c * sub + n] for n in range(sub)]
        return [_pv_stage(vt, p_sc.at[2 * slot + mi], acc_sc.at[mi], None if alphas is None else alphas[mi])
                for mi in range(2)]

    def restart():
        m_sc[...] = jnp.full_like(m_sc, NEG_INF)

    def finish(totals):
        lamp = lam_ref[...]
        linit = linit_ref[...]
        lam = (jnp.exp(jnp.sum(lamp[0:1] * lamp[1:2], axis=-1, keepdims=True))
               - jnp.exp(jnp.sum(lamp[2:3] * lamp[3:4], axis=-1, keepdims=True)) + linit)
        nv = 2 * DIFF_DIM
        a1, a2 = totals
        o = a1[:nv] / a1[nv:nv + 1] - lam * (a2[:nv] / a2[nv:nv + 1])
        o = _rms_norm(o.T, g_ref[...]) * (1.0 - linit)
        o_ref[0] = o.astype(o_ref.dtype)
        acc_sc[...] = jnp.zeros_like(acc_sc)

    _tile_pipeline(qi, q1t_ref.shape[3] // tq, nk, setup=setup, qk=qk, sm=sm, pv=pv, restart=restart, finish=finish)


def _diff_attention(layer, q1t, q2t, k1, k2, vt, lamp, linit, subln_g):
    b, hds, _, s = q1t.shape
    nc = vt.shape[2]
    t = DIFF_TILE
    assert t % KV_CHUNK == 0 and s % (2 * t) == 0
    f32, bf = jnp.float32, jnp.bfloat16
    qspec = pl.BlockSpec((1, 1, LANE, s), lambda bb, h, i: (bb, h, 0, 0))
    kspec = pl.BlockSpec((1, 1, s, LANE), lambda bb, h, i: (bb, h, 0, 0))
    const = lambda shape: _layer_block(shape, layer)
    return pl.pallas_call(
        _diff_kernel,
        grid=(b, hds, s // t),
        in_specs=[
            qspec, qspec, kspec, kspec,
            pl.BlockSpec((1, 1, nc, V_ROWS, KV_CHUNK), lambda bb, h, i: (bb, h, 0, 0, 0)),
            const((4, DIFF_DIM)), const((1, 1)), const((1, 2 * DIFF_DIM)),
        ],
        out_specs=pl.BlockSpec((1, t, 2 * DIFF_DIM), lambda bb, h, i: (bb, i, h)),
        out_shape=jax.ShapeDtypeStruct((b, s, hds * 2 * DIFF_DIM), bf),
        scratch_shapes=[
            pltpu.VMEM((4, LANE, t), bf),
            pltpu.VMEM((t, t), f32),
            pltpu.VMEM((4, t, t), f32),
            pltpu.VMEM((4, t, t), bf),
            pltpu.VMEM((4, 1, t), f32),
            pltpu.VMEM((2, 1, t), f32),
            pltpu.VMEM((2, V_ROWS, t), f32),
        ],
        compiler_params=pltpu.CompilerParams(
            dimension_semantics=("parallel", "parallel", "arbitrary"), vmem_limit_bytes=VMEM_LIMIT),
        name="diff_attn",
    )(q1t, q2t, k1, k2, vt, lamp, linit, subln_g)


def _swap_halves(w):
    half = w.shape[-1] // 2
    return jnp.concatenate([w[..., half:], w[..., :half]], axis=-1)


def _prep_w_in(w_in):
    w = w_in.astype(jnp.bfloat16)
    rope_end = MLA_Q_RANK + MLA_KV_RANK + MLA_ROPE
    k_rope = w[..., rope_end - MLA_ROPE:rope_end]
    return jnp.concatenate([w[..., :rope_end], _swap_halves(k_rope), w[..., rope_end:]], axis=-1)


def _prep_w_q_up(w_q_up):
    l, r, _ = w_q_up.shape
    w = w_q_up.astype(jnp.bfloat16).reshape(l, r, MLA_HEADS, MLA_NOPE + MLA_ROPE)
    w = jnp.concatenate([w, _swap_halves(w[..., MLA_NOPE:])], axis=-1)
    return w.reshape(l, r, MLA_HEADS * MLA_QK)


def _rope_table(s):
    inv_freq = np.float32(ROPE_THETA) ** (-np.arange(0, MLA_ROPE, 2, dtype=np.float32) / np.float32(MLA_ROPE))
    ang = np.arange(s, dtype=np.float32)[:, None] * inv_freq[None, :]
    cos, sin = np.cos(ang), np.sin(ang)
    return jnp.asarray(np.concatenate([cos, cos, -sin, sin], axis=-1), jnp.float32)


def kernel(x, ffn1_w_gate, ffn1_w_up, ffn1_w_down, ln1_g, ln1_b, w_in, q_norm_g, w_q_up, kv_norm_g, w_kv_up, diff_lambda_q1, diff_lambda_k1, diff_lambda_q2, diff_lambda_k2, diff_subln_g, w_out, ln2_g, ln2_b, ffn2_w_gate, ffn2_w_up, ffn2_w_down, ln3_g, ln3_b):
    b, s, d = x.shape
    bf = jnp.bfloat16
    rope = _rope_table(s)
    w_in_p = _prep_w_in(w_in)
    w_q_p = _prep_w_q_up(w_q_up)
    w_kv_p = w_kv_up.astype(bf)
    w_out_p = w_out.astype(bf)
    f1 = (ffn1_w_gate.astype(bf), ffn1_w_up.astype(bf), ffn1_w_down.astype(bf))
    f2 = (ffn2_w_gate.astype(bf), ffn2_w_up.astype(bf), ffn2_w_down.astype(bf))
    lamp = jnp.stack([diff_lambda_q1, diff_lambda_k1, diff_lambda_q2, diff_lambda_k2], axis=1)
    linit = jnp.asarray([[[0.8 - 0.6 * math.exp(-0.3 * l)]] for l in range(DEPTH)], jnp.float32)
    rows = lambda v: v[:, None, :]

    xf = x.reshape(b * s, d)
    for l in range(DEPTH):
        xf = _ffn_ln(l, xf, *f1, rows(ln1_g), rows(ln1_b))
        qt, km, vtm, q1t, q2t, k1, k2, vtd = _inproj(
            l, xf.reshape(b, s, d), w_in_p, rows(q_norm_g), w_q_p, rows(kv_norm_g), w_kv_p, rope)
        o_mla = _mla_attention(qt, km, vtm)
        o_diff = _diff_attention(l, q1t, q2t, k1, k2, vtd, lamp, linit, rows(diff_subln_g))
        xf = _mix_ffn_ln(l, xf, o_mla.reshape(b * s, -1), o_diff.reshape(b * s, -1), w_out_p,
                         rows(ln2_g), rows(ln2_b), *f2, rows(ln3_g), rows(ln3_b))
    return xf.reshape(b, s, d)
```

```python
import math

import jax
import jax.numpy as jnp
import numpy as np
from jax import lax
from jax.experimental import pallas as pl
from jax.experimental.pallas import tpu as pltpu

D_MODEL = 1024
DEPTH = 4
MLA_HEADS = 4
MLA_NOPE = 128
MLA_ROPE = 64
MLA_V = 128
MLA_Q_RANK = 256
MLA_KV_RANK = 128
DIFF_HEADS = 4
DIFF_DIM = 64
D_FF = 2816
ROPE_THETA = 10000.0
NORM_EPS = 1e-5
ALPHA = (2 * DEPTH) ** 0.25

LANE = 128
ROW_TILE = 512
KV_CHUNK = ROW_TILE
DIFF_TILE = 1024
MLA_Q_TILE = 2048
MLA_K_TILE = 1024
POS_SPLIT = 128
VMEM_V7X = 64 * 1024 * 1024
VMEM_LIMIT = VMEM_V7X - 6 * 1024 * 1024

MLA_QK = MLA_NOPE + 2 * MLA_ROPE
MLA_KV = MLA_NOPE + MLA_V
DIFF_W = DIFF_HEADS * 2 * DIFF_DIM
W_IN_SPLITS = (MLA_Q_RANK, MLA_KV_RANK, 2 * MLA_ROPE, DIFF_W, DIFF_W, DIFF_W)
W_IN_COLS = sum(W_IN_SPLITS)
NEG_INF = float("-inf")
LOG2E = math.log2(math.e)
AUG_ROWS = 16
V_ROWS = MLA_V + AUG_ROWS


def _bf16_pieces(x, n):
    out = []
    for _ in range(n):
        piece = float(np.asarray(x, dtype=jnp.bfloat16).astype(np.float32))
        out.append(piece)
        x -= piece
    return tuple(out)


LOG2E_PIECES = _bf16_pieces(LOG2E, 3)
ALIBI_TERMS = 4
ALIBI_SLOTS = len(LOG2E_PIECES) * ALIBI_TERMS
assert ALIBI_SLOTS <= AUG_ROWS


def _dot(a, b):
    return jnp.dot(a, b, preferred_element_type=jnp.float32)


def _layer_norm(y, g, b):
    mu = jnp.mean(y, axis=-1, keepdims=True)
    d = y - mu
    var = jnp.mean(d * d, axis=-1, keepdims=True)
    return d * lax.rsqrt(var + NORM_EPS) * g + b


def _rms_norm(y, g):
    return y * lax.rsqrt(jnp.mean(y * y, axis=-1, keepdims=True) + NORM_EPS) * g


def _ffn_ln_rows(x, wg_ref, wu_ref, wd_ref, g_ref, b_ref):
    xb = x.astype(jnp.bfloat16)
    gate = _dot(xb, wg_ref[...])
    up = _dot(xb, wu_ref[...])
    h = gate / (1.0 + jnp.exp(-gate)) * up
    ffn = _dot(h.astype(jnp.bfloat16), wd_ref[...])
    return _layer_norm(ALPHA * x + 0.5 * ffn, g_ref[...], b_ref[...])


def _row_halves(n):
    return [slice(0, n // 2), slice(n // 2, n)]


def _ffn_ln_kernel(x_ref, wg_ref, wu_ref, wd_ref, g_ref, b_ref, o_ref):
    for rows in _row_halves(x_ref.shape[0]):
        o_ref[rows, :] = _ffn_ln_rows(x_ref[rows, :], wg_ref, wu_ref, wd_ref, g_ref, b_ref)


def _mix_ffn_ln_kernel(x_ref, om_ref, od_ref, wo_ref, g2_ref, b2_ref,
                       wg_ref, wu_ref, wd_ref, g3_ref, b3_ref, o_ref):
    nm = om_ref.shape[1]
    mix = _dot(om_ref[...], wo_ref[:nm, :]) + _dot(od_ref[...], wo_ref[nm:, :])
    x = _layer_norm(ALPHA * x_ref[...] + mix, g2_ref[...], b2_ref[...])
    o_ref[...] = _ffn_ln_rows(x, wg_ref, wu_ref, wd_ref, g3_ref, b3_ref)


def _row_tile(width):
    return pl.BlockSpec((ROW_TILE, width), lambda i: (i, 0))


def _layer_block(shape, layer):
    return pl.BlockSpec((None,) + tuple(shape), lambda *_: (layer,) + (0,) * len(shape),
                        pipeline_mode=pl.Buffered(1))


def _ffn_specs(layer):
    return [_layer_block((D_MODEL, D_FF), layer), _layer_block((D_MODEL, D_FF), layer),
            _layer_block((D_FF, D_MODEL), layer), _layer_block((1, D_MODEL), layer), _layer_block((1, D_MODEL), layer)]


def _ffn_ln(layer, x, wg, wu, wd, g, b):
    t = x.shape[0]
    return pl.pallas_call(
        _ffn_ln_kernel,
        grid=(t // ROW_TILE,),
        in_specs=[_row_tile(D_MODEL)] + _ffn_specs(layer),
        out_specs=_row_tile(D_MODEL),
        out_shape=jax.ShapeDtypeStruct((t, D_MODEL), jnp.float32),
        compiler_params=pltpu.CompilerParams(
            dimension_semantics=("parallel",), vmem_limit_bytes=VMEM_LIMIT),
        name="ffn_ln",
    )(x, wg, wu, wd, g, b)


def _mix_ffn_ln(layer, x, om, od, wo, g2, b2, wg, wu, wd, g3, b3):
    t = x.shape[0]
    nm, nd = om.shape[1], od.shape[1]
    return pl.pallas_call(
        _mix_ffn_ln_kernel,
        grid=(t // ROW_TILE,),
        in_specs=[_row_tile(D_MODEL), _row_tile(nm), _row_tile(nd), _layer_block((nm + nd, D_MODEL), layer),
                  _layer_block((1, D_MODEL), layer), _layer_block((1, D_MODEL), layer)] + _ffn_specs(layer),
        out_specs=_row_tile(D_MODEL),
        out_shape=jax.ShapeDtypeStruct((t, D_MODEL), jnp.float32),
        compiler_params=pltpu.CompilerParams(
            dimension_semantics=("parallel",), vmem_limit_bytes=VMEM_LIMIT),
        name="mix_ffn_ln",
    )(x, om, od, wo, g2, b2, wg, wu, wd, g3, b3)


def _alibi_slope(h):
    return 2.0 ** (-8.0 * (h + 1) / DIFF_HEADS)


def _inproj_kernel(x_ref, win_ref, qg_ref, wq_ref, kvg_ref, wkv_ref, rope_ref,
                   qt_ref, km_ref, vtm_ref, q1t_ref, q2t_ref, k1_ref, k2_ref, vtd_ref):
    i = pl.program_id(1)
    tm = x_ref.shape[1]
    xb = x_ref[0].astype(jnp.bfloat16)
    h = _dot(xb, win_ref[...])
    edges = np.cumsum((0,) + W_IN_SPLITS)
    c_q, c_kv, kr, dq, dk, dv = (h[:, int(a):int(b)] for a, b in zip(edges[:-1], edges[1:]))
    rope = rope_ref[...]

    q = _dot(_rms_norm(c_q, qg_ref[...]).astype(jnp.bfloat16), wq_ref[...])
    kv = _dot(_rms_norm(c_kv, kvg_ref[...]).astype(jnp.bfloat16), wkv_ref[...])
    krt = kr * rope
    k_rope = krt + pltpu.roll(krt, MLA_ROPE, axis=1)
    mla_scale = (MLA_NOPE + MLA_ROPE) ** -0.5 * LOG2E
    ones_rows = jnp.where(lax.broadcasted_iota(jnp.int32, (AUG_ROWS, tm), 0) == 0, 1.0, 0.0)
    for hh in range(MLA_HEADS):
        qh = q[:, hh * MLA_QK:(hh + 1) * MLA_QK]
        qh = jnp.concatenate([qh[:, :MLA_NOPE], qh[:, MLA_NOPE:] * rope], axis=1) * mla_scale
        qt_ref[0, hh] = qh.T.astype(jnp.bfloat16)
        kvh = kv[:, hh * MLA_KV:(hh + 1) * MLA_KV]
        km_ref[0, hh] = jnp.concatenate([kvh[:, :MLA_NOPE], k_rope], axis=1).astype(jnp.bfloat16)
        vtm_ref[0, hh, 0] = jnp.concatenate([kvh[:, MLA_NOPE:].T, ones_rows], axis=0).astype(jnp.bfloat16)

    pos_c = i * tm + lax.broadcasted_iota(jnp.int32, (AUG_ROWS, tm), 1)
    row = lax.broadcasted_iota(jnp.int32, (AUG_ROWS, tm), 0)
    hi_c = (pos_c & -POS_SPLIT).astype(jnp.float32)
    lo_c = (pos_c & (POS_SPLIT - 1)).astype(jnp.float32)
    pos_r = i * tm + lax.broadcasted_iota(jnp.int32, (tm, LANE), 0)
    lane = lax.broadcasted_iota(jnp.int32, (tm, LANE), 1)
    hi_r = (pos_r & -POS_SPLIT).astype(jnp.float32)
    lo_r = (pos_r & (POS_SPLIT - 1)).astype(jnp.float32)
    diff_scale = DIFF_DIM ** -0.5 * LOG2E
    zpad = jnp.zeros((DIFF_DIM - AUG_ROWS, tm), jnp.float32)
    def piece(slot):
        return jnp.where(slot < ALIBI_TERMS, LOG2E_PIECES[0],
                         jnp.where(slot < 2 * ALIBI_TERMS, LOG2E_PIECES[1], LOG2E_PIECES[2]))

    slot_r = lane - DIFF_DIM
    piece_c, term_c = piece(row), row & (ALIBI_TERMS - 1)
    piece_r, term_r = piece(slot_r), slot_r & (ALIBI_TERMS - 1)
    for hh in range(DIFF_HEADS):
        slope = _alibi_slope(hh)
        qaug = jnp.where(row >= ALIBI_SLOTS, 0.0,
                         jnp.where(term_c == 0, -slope * hi_c,
                                   jnp.where(term_c == 1, -slope * lo_c, piece_c)))
        kaug = jnp.where(slot_r >= ALIBI_SLOTS, 0.0,
                         jnp.where(term_r < 2, piece_r,
                                   jnp.where(term_r == 2, slope * hi_r, slope * lo_r)))
        per_head = slice(hh * 2 * DIFF_DIM, (hh + 1) * 2 * DIFF_DIM)
        tq = (dq[:, per_head] * diff_scale).T
        q1t_ref[0, hh] = jnp.concatenate([tq[:DIFF_DIM], qaug, zpad], axis=0).astype(jnp.bfloat16)
        q2t_ref[0, hh] = jnp.concatenate([tq[DIFF_DIM:], qaug, zpad], axis=0).astype(jnp.bfloat16)
        tk = dk[:, per_head]
        k1_ref[0, hh] = jnp.where(lane < DIFF_DIM, tk, kaug).astype(jnp.bfloat16)
        k2_ref[0, hh] = jnp.where(lane < DIFF_DIM, pltpu.roll(tk, DIFF_DIM, axis=1), kaug).astype(jnp.bfloat16)
        vtd_ref[0, hh, 0] = jnp.concatenate(
            [dv[:, per_head].T, ones_rows], axis=0).astype(jnp.bfloat16)


def _inproj(layer, x, win, qg, wq, kvg, wkv, rope):
    b, s, _ = x.shape
    nc = s // KV_CHUNK
    tm = ROW_TILE
    bf = jnp.bfloat16
    const = lambda shape: _layer_block(shape, layer)
    out_shape = (
        jax.ShapeDtypeStruct((b, MLA_HEADS, MLA_QK, s), bf),
        jax.ShapeDtypeStruct((b, MLA_HEADS, s, MLA_QK), bf),
        jax.ShapeDtypeStruct((b, MLA_HEADS, nc, V_ROWS, KV_CHUNK), bf),
        jax.ShapeDtypeStruct((b, DIFF_HEADS, LANE, s), bf),
        jax.ShapeDtypeStruct((b, DIFF_HEADS, LANE, s), bf),
        jax.ShapeDtypeStruct((b, DIFF_HEADS, s, LANE), bf),
        jax.ShapeDtypeStruct((b, DIFF_HEADS, s, LANE), bf),
        jax.ShapeDtypeStruct((b, DIFF_HEADS, nc, V_ROWS, KV_CHUNK), bf),
    )
    out_specs = (
        pl.BlockSpec((1, MLA_HEADS, MLA_QK, tm), lambda bb, i: (bb, 0, 0, i)),
        pl.BlockSpec((1, MLA_HEADS, tm, MLA_QK), lambda bb, i: (bb, 0, i, 0)),
        pl.BlockSpec((1, MLA_HEADS, 1, V_ROWS, tm), lambda bb, i: (bb, 0, i, 0, 0)),
        pl.BlockSpec((1, DIFF_HEADS, LANE, tm), lambda bb, i: (bb, 0, 0, i)),
        pl.BlockSpec((1, DIFF_HEADS, LANE, tm), lambda bb, i: (bb, 0, 0, i)),
        pl.BlockSpec((1, DIFF_HEADS, tm, LANE), lambda bb, i: (bb, 0, i, 0)),
        pl.BlockSpec((1, DIFF_HEADS, tm, LANE), lambda bb, i: (bb, 0, i, 0)),
        pl.BlockSpec((1, DIFF_HEADS, 1, V_ROWS, tm), lambda bb, i: (bb, 0, i, 0, 0)),
    )
    return pl.pallas_call(
        _inproj_kernel,
        grid=(b, s // tm),
        in_specs=[
            pl.BlockSpec((1, tm, D_MODEL), lambda bb, i: (bb, i, 0)),
            const((D_MODEL, W_IN_COLS)),
            const((1, MLA_Q_RANK)),
            const((MLA_Q_RANK, MLA_HEADS * MLA_QK)),
            const((1, MLA_KV_RANK)),
            const((MLA_KV_RANK, MLA_HEADS * MLA_KV)),
            pl.BlockSpec((tm, LANE), lambda bb, i: (i, 0)),
        ],
        out_specs=out_specs,
        out_shape=out_shape,
        compiler_params=pltpu.CompilerParams(
            dimension_semantics=("parallel", "parallel"), vmem_limit_bytes=VMEM_LIMIT),
        name="inproj",
    )(x, win, qg, wq, kvg, wkv, rope)


def _qk_stage(k_tile, q_tile, s_ref, mb_ref, corr=None):
    s = _dot(k_tile, q_tile)
    if corr is not None:
        s = s - corr
    s_ref[...] = s
    mb_ref[...] = jnp.max(s, axis=0, keepdims=True)


def _sm_stage(s_ref, mb_ref, p_ref, m_ref):
    m_old = m_ref[...]
    m_new = jnp.maximum(m_old, mb_ref[...])
    p_ref[...] = jnp.exp2(s_ref[...] - m_new).astype(p_ref.dtype)
    m_ref[...] = m_new
    return jnp.exp2(m_old - m_new)


def _pv_stage(vt_tiles, p_ref, acc_ref, alpha):
    acc = acc_ref[...]
    for n, vt_tile in enumerate(vt_tiles):
        acc = acc + _dot(vt_tile, p_ref[n * KV_CHUNK:(n + 1) * KV_CHUNK, :])
    if alpha is None:
        return acc
    acc_ref[...] = alpha * acc


def _key_rows(c, tk):
    return pl.ds(pl.multiple_of(c * tk, tk), tk)


def _query_cols(tile, tq):
    return pl.ds(pl.multiple_of(tile * tq, tq), tq)


def _tile_pipeline(qi, nq, nk, *, setup, qk, sm, pv, restart, finish):
    @pl.when(qi == 0)
    def _():
        setup()
        restart()
        qk(0, 0, 0)
        qk(0, 1, 1)
        sm(0)

    def step(j, slot):
        qk(qi, j + 1, 1 - slot)
        pv(j - 1, 1 - slot, sm(slot))

    def pair(t, carry):
        step(2 * t + 1, 1)
        step(2 * t + 2, 0)
        return carry

    lax.fori_loop(0, nk // 2 - 1, pair, 0)

    @pl.when(qi < nq - 1)
    def _():
        qk(qi + 1, 0, 0)
        pv(nk - 2, 0, sm(1))
        qk(qi + 1, 1, 1)
        restart()
        sm(0)
        finish(pv(nk - 1, 1, None))

    @pl.when(qi == nq - 1)
    def _():
        pv(nk - 2, 0, sm(1))
        finish(pv(nk - 1, 1, None))


def _mla_kernel(qt_ref, k_ref, vt_ref, o_ref, s0, s1, p0, p1, mb0, mb1, m_sc, acc_sc):
    qi = pl.program_id(2)
    tk, tq = s0.shape
    sub = tk // KV_CHUNK
    nk = vt_ref.shape[2] // sub
    s_bufs, p_bufs, mb_bufs = (s0, s1), (p0, p1), (mb0, mb1)
    halves = [pl.ds(i * (tq // 2), tq // 2) for i in range(2)]

    def setup():
        acc_sc[...] = jnp.zeros_like(acc_sc)

    def qk(tile, j, slot):
        for i, hv in enumerate(halves):
            cols = pl.ds(pl.multiple_of(tile * tq + i * (tq // 2), tq // 2), tq // 2)
            _qk_stage(k_ref[0, 0, _key_rows(j, tk), :], qt_ref[0, 0, :, cols],
                      s_bufs[slot].at[:, hv], mb_bufs[slot].at[:, hv])

    def sm(slot):
        return [_sm_stage(s_bufs[slot].at[:, hv], mb_bufs[slot].at[:, hv], p_bufs[slot].at[:, hv],
                          m_sc.at[:, hv]) for hv in halves]

    def pv(j, slot, alphas):
        vt = [vt_ref[0, 0, j * sub + n] for n in range(sub)]
        return [_pv_stage(vt, p_bufs[slot].at[:, hv], acc_sc.at[:, hv], None if alphas is None else alphas[i])
                for i, hv in enumerate(halves)]

    def restart():
        m_sc[...] = jnp.full_like(m_sc, NEG_INF)

    def finish(totals):
        acc = jnp.concatenate(totals, axis=1)
        o = acc[:MLA_V] / acc[MLA_V:MLA_V + 1]
        o_ref[0] = o.T.astype(o_ref.dtype)
        acc_sc[...] = jnp.zeros_like(acc_sc)

    _tile_pipeline(qi, qt_ref.shape[3] // tq, nk, setup=setup, qk=qk, sm=sm, pv=pv, restart=restart, finish=finish)


def _mla_attention(qt, k, vt):
    b, hds, _, s = qt.shape
    nc = vt.shape[2]
    tq, tk = MLA_Q_TILE, MLA_K_TILE
    assert tk % KV_CHUNK == 0 and s % (2 * tk) == 0 and s % tq == 0
    f32, bf = jnp.float32, jnp.bfloat16
    return pl.pallas_call(
        _mla_kernel,
        grid=(b, hds, s // tq),
        in_specs=[
            pl.BlockSpec((1, 1, MLA_QK, s), lambda bb, h, i: (bb, h, 0, 0)),
            pl.BlockSpec((1, 1, s, MLA_QK), lambda bb, h, i: (bb, h, 0, 0)),
            pl.BlockSpec((1, 1, nc, V_ROWS, KV_CHUNK), lambda bb, h, i: (bb, h, 0, 0, 0)),
        ],
        out_specs=pl.BlockSpec((1, tq, MLA_V), lambda bb, h, i: (bb, i, h)),
        out_shape=jax.ShapeDtypeStruct((b, s, hds * MLA_V), bf),
        scratch_shapes=[
            pltpu.VMEM((tk, tq), f32), pltpu.VMEM((tk, tq), f32),
            pltpu.VMEM((tk, tq), bf), pltpu.VMEM((tk, tq), bf),
            pltpu.VMEM((1, tq), f32), pltpu.VMEM((1, tq), f32),
            pltpu.VMEM((1, tq), f32),
            pltpu.VMEM((V_ROWS, tq), f32),
        ],
        compiler_params=pltpu.CompilerParams(
            dimension_semantics=("parallel", "parallel", "arbitrary"), vmem_limit_bytes=VMEM_LIMIT),
        name="mla_attn",
    )(qt, k, vt)


def _diff_kernel(q1t_ref, q2t_ref, k1_ref, k2_ref, vt_ref, lam_ref, linit_ref, g_ref, o_ref,
                 qv_sc, corr_sc, s_sc, p_sc, mb_sc, m_sc, acc_sc):
    h = pl.program_id(1)
    qi = pl.program_id(2)
    tq = o_ref.shape[1]
    sub = tq // KV_CHUNK
    nk = vt_ref.shape[2] // sub
    k_refs = (k1_ref, k2_ref)
    q_refs = (q1t_ref, q2t_ref)

    def setup():
        jj = lax.broadcasted_iota(jnp.int32, (tq, tq), 0)
        ii = lax.broadcasted_iota(jnp.int32, (tq, tq), 1)
        expo = jnp.full((1, tq), 127 + 1, jnp.int32) - 2 * (h + 1)
        slope2 = lax.bitcast_convert_type(expo << 23, jnp.float32)
        corr_sc[...] = (slope2 * LOG2E) * jnp.maximum(jj - ii, 0).astype(jnp.float32)
        acc_sc[...] = jnp.zeros_like(acc_sc)

    def chunk(tile, j):
        return lax.rem(tile + j, nk)

    def load_queries(tile):
        row = lax.broadcasted_iota(jnp.int32, (LANE, tq), 0)
        for mi in range(2):
            qq = q_refs[mi][0, 0, :, _query_cols(tile, tq)]
            qv_sc[mi] = qq
            qv_sc[2 + mi] = jnp.where(row < DIFF_DIM, qq, -qq)

    def qk(tile, j, slot):
        diagonal = isinstance(j, int) and j == 0
        if diagonal:
            load_queries(tile)
        c = chunk(tile, j)
        variant = 0 if diagonal else jnp.where(c > tile, 2, 0)
        for mi in range(2):
            _qk_stage(k_refs[mi][0, 0, _key_rows(c, tq), :], qv_sc[variant + mi],
                      s_sc.at[2 * slot + mi], mb_sc.at[2 * slot + mi],
                      corr_sc[...] if diagonal else None)

    def sm(slot):
        return [_sm_stage(s_sc.at[2 * slot + mi], mb_sc.at[2 * slot + mi], p_sc.at[2 * slot + mi],
                          m_sc.at[mi]) for mi in range(2)]

    def pv(j, slot, alphas):
        c = chunk(qi, j)
        vt = [vt_ref[0, 0, c * sub + n] for n in range(sub)]
        return [_pv_stage(vt, p_sc.at[2 * slot + mi], acc_sc.at[mi], None if alphas is None else alphas[mi])
                for mi in range(2)]

    def restart():
        m_sc[...] = jnp.full_like(m_sc, NEG_INF)

    def finish(totals):
        lamp = lam_ref[...]
        linit = linit_ref[...]
        lam = (jnp.exp(jnp.sum(lamp[0:1] * lamp[1:2], axis=-1, keepdims=True))
               - jnp.exp(jnp.sum(lamp[2:3] * lamp[3:4], axis=-1, keepdims=True)) + linit)
        nv = 2 * DIFF_DIM
        a1, a2 = totals
        o = a1[:nv] * (1.0 / a1[nv:nv + 1]) - a2[:nv] * (lam / a2[nv:nv + 1])
        o = o * lax.rsqrt(jnp.mean(o * o, axis=0, keepdims=True) + NORM_EPS)
        o_ref[0] = (o.T * (g_ref[...] * (1.0 - linit))).astype(o_ref.dtype)
        acc_sc[...] = jnp.zeros_like(acc_sc)

    _tile_pipeline(qi, q1t_ref.shape[3] // tq, nk, setup=setup, qk=qk, sm=sm, pv=pv, restart=restart, finish=finish)


def _diff_attention(layer, q1t, q2t, k1, k2, vt, lamp, linit, subln_g):
    b, hds, _, s = q1t.shape
    nc = vt.shape[2]
    t = DIFF_TILE
    assert t % KV_CHUNK == 0 and s % (2 * t) == 0
    f32, bf = jnp.float32, jnp.bfloat16
    qspec = pl.BlockSpec((1, 1, LANE, s), lambda bb, h, i: (bb, h, 0, 0))
    kspec = pl.BlockSpec((1, 1, s, LANE), lambda bb, h, i: (bb, h, 0, 0))
    const = lambda shape: _layer_block(shape, layer)
    return pl.pallas_call(
        _diff_kernel,
        grid=(b, hds, s // t),
        in_specs=[
            qspec, qspec, kspec, kspec,
            pl.BlockSpec((1, 1, nc, V_ROWS, KV_CHUNK), lambda bb, h, i: (bb, h, 0, 0, 0)),
            const((4, DIFF_DIM)), const((1, 1)), const((1, 2 * DIFF_DIM)),
        ],
        out_specs=pl.BlockSpec((1, t, 2 * DIFF_DIM), lambda bb, h, i: (bb, i, h)),
        out_shape=jax.ShapeDtypeStruct((b, s, hds * 2 * DIFF_DIM), bf),
        scratch_shapes=[
            pltpu.VMEM((4, LANE, t), bf),
            pltpu.VMEM((t, t), f32),
            pltpu.VMEM((4, t, t), f32),
            pltpu.VMEM((4, t, t), bf),
            pltpu.VMEM((4, 1, t), f32),
            pltpu.VMEM((2, 1, t), f32),
            pltpu.VMEM((2, V_ROWS, t), f32),
        ],
        compiler_params=pltpu.CompilerParams(
            dimension_semantics=("parallel", "parallel", "arbitrary"), vmem_limit_bytes=VMEM_LIMIT),
        name="diff_attn",
    )(q1t, q2t, k1, k2, vt, lamp, linit, subln_g)


def _swap_halves(w):
    half = w.shape[-1] // 2
    return jnp.concatenate([w[..., half:], w[..., :half]], axis=-1)


def _prep_w_in(w_in):
    w = w_in.astype(jnp.bfloat16)
    rope_end = MLA_Q_RANK + MLA_KV_RANK + MLA_ROPE
    k_rope = w[..., rope_end - MLA_ROPE:rope_end]
    return jnp.concatenate([w[..., :rope_end], _swap_halves(k_rope), w[..., rope_end:]], axis=-1)


def _prep_w_q_up(w_q_up):
    l, r, _ = w_q_up.shape
    w = w_q_up.astype(jnp.bfloat16).reshape(l, r, MLA_HEADS, MLA_NOPE + MLA_ROPE)
    w = jnp.concatenate([w, _swap_halves(w[..., MLA_NOPE:])], axis=-1)
    return w.reshape(l, r, MLA_HEADS * MLA_QK)


def _rope_table(s):
    inv_freq = np.float32(ROPE_THETA) ** (-np.arange(0, MLA_ROPE, 2, dtype=np.float32) / np.float32(MLA_ROPE))
    ang = np.arange(s, dtype=np.float32)[:, None] * inv_freq[None, :]
    cos, sin = np.cos(ang), np.sin(ang)
    return jnp.asarray(np.concatenate([cos, cos, -sin, sin], axis=-1), jnp.float32)


def kernel(x, ffn1_w_gate, ffn1_w_up, ffn1_w_down, ln1_g, ln1_b, w_in, q_norm_g, w_q_up, kv_norm_g, w_kv_up, diff_lambda_q1, diff_lambda_k1, diff_lambda_q2, diff_lambda_k2, diff_subln_g, w_out, ln2_g, ln2_b, ffn2_w_gate, ffn2_w_up, ffn2_w_down, ln3_g, ln3_b):
    b, s, d = x.shape
    bf = jnp.bfloat16
    rope = _rope_table(s)
    w_in_p = _prep_w_in(w_in)
    w_q_p = _prep_w_q_up(w_q_up)
    w_kv_p = w_kv_up.astype(bf)
    w_out_p = w_out.astype(bf)
    f1 = (ffn1_w_gate.astype(bf), ffn1_w_up.astype(bf), ffn1_w_down.astype(bf))
    f2 = (ffn2_w_gate.astype(bf), ffn2_w_up.astype(bf), ffn2_w_down.astype(bf))
    lamp = jnp.stack([diff_lambda_q1, diff_lambda_k1, diff_lambda_q2, diff_lambda_k2], axis=1)
    linit = jnp.asarray([[[0.8 - 0.6 * math.exp(-0.3 * l)]] for l in range(DEPTH)], jnp.float32)
    rows = lambda v: v[:, None, :]

    xf = x.reshape(b * s, d)
    for l in range(DEPTH):
        xf = _ffn_ln(l, xf, *f1, rows(ln1_g), rows(ln1_b))
        qt, km, vtm, q1t, q2t, k1, k2, vtd = _inproj(
            l, xf.reshape(b, s, d), w_in_p, rows(q_norm_g), w_q_p, rows(kv_norm_g), w_kv_p, rope)
        o_mla = _mla_attention(qt, km, vtm)
        o_diff = _diff_attention(l, q1t, q2t, k1, k2, vtd, lamp, linit, rows(diff_subln_g))
        xf = _mix_ffn_ln(l, xf, o_mla.reshape(b * s, -1), o_diff.reshape(b * s, -1), w_out_p,
                         rows(ln2_g), rows(ln2_b), *f2, rows(ln3_g), rows(ln3_b))
    return xf.reshape(b, s, d)
```

```python
import math

import jax
import jax.numpy as jnp
import numpy as np
from jax import lax
from jax.experimental import pallas as pl
from jax.experimental.pallas import tpu as pltpu

D_MODEL = 1024
DEPTH = 4
MLA_HEADS = 4
MLA_NOPE = 128
MLA_ROPE = 64
MLA_V = 128
MLA_Q_RANK = 256
MLA_KV_RANK = 128
DIFF_HEADS = 4
DIFF_DIM = 64
D_FF = 2816
ROPE_THETA = 10000.0
NORM_EPS = 1e-5
ALPHA = (2 * DEPTH) ** 0.25

LANE = 128
ROW_TILE = 512
KV_CHUNK = ROW_TILE
DIFF_TILE = 1024
MLA_Q_TILE = 2048
MLA_K_TILE = 1024
POS_SPLIT = 128
VMEM_V7X = 64 * 1024 * 1024
VMEM_LIMIT = VMEM_V7X - 6 * 1024 * 1024

MLA_QK = MLA_NOPE + 2 * MLA_ROPE
MLA_KV = MLA_NOPE + MLA_V
DIFF_W = DIFF_HEADS * 2 * DIFF_DIM
W_IN_SPLITS = (MLA_Q_RANK, MLA_KV_RANK, 2 * MLA_ROPE, DIFF_W, DIFF_W, DIFF_W)
W_IN_COLS = sum(W_IN_SPLITS)
NEG_INF = float("-inf")
LOG2E = math.log2(math.e)
AUG_ROWS = 16
V_ROWS = MLA_V + AUG_ROWS


def _bf16_pieces(x, n):
    out = []
    for _ in range(n):
        piece = float(np.asarray(x, dtype=jnp.bfloat16).astype(np.float32))
        out.append(piece)
        x -= piece
    return tuple(out)


LOG2E_PIECES = _bf16_pieces(LOG2E, 3)
ALIBI_TERMS = 4
ALIBI_SLOTS = len(LOG2E_PIECES) * ALIBI_TERMS
assert ALIBI_SLOTS <= AUG_ROWS


def _dot(a, b):
    return jnp.dot(a, b, preferred_element_type=jnp.float32)


def _layer_norm(y, g, b):
    mu = jnp.mean(y, axis=-1, keepdims=True)
    d = y - mu
    var = jnp.mean(d * d, axis=-1, keepdims=True)
    return d * lax.rsqrt(var + NORM_EPS) * g + b


def _rms_norm(y, g):
    return y * lax.rsqrt(jnp.mean(y * y, axis=-1, keepdims=True) + NORM_EPS) * g


def _ffn_ln_rows(x, wg_ref, wu_ref, wd_ref, g_ref, b_ref):
    xb = x.astype(jnp.bfloat16)
    gate = _dot(xb, wg_ref[...])
    up = _dot(xb, wu_ref[...])
    h = gate / (1.0 + jnp.exp(-gate)) * up
    ffn = _dot(h.astype(jnp.bfloat16), wd_ref[...])
    return _layer_norm(ALPHA * x + 0.5 * ffn, g_ref[...], b_ref[...])


def _row_halves(n):
    return [slice(0, n // 2), slice(n // 2, n)]


def _ffn_ln_kernel(x_ref, wg_ref, wu_ref, wd_ref, g_ref, b_ref, o_ref):
    for rows in _row_halves(x_ref.shape[0]):
        o_ref[rows, :] = _ffn_ln_rows(x_ref[rows, :], wg_ref, wu_ref, wd_ref, g_ref, b_ref)


def _mix_ffn_ln_kernel(x_ref, om_ref, od_ref, wo_ref, g2_ref, b2_ref,
                       wg_ref, wu_ref, wd_ref, g3_ref, b3_ref, o_ref):
    nm = om_ref.shape[1]
    mix = _dot(om_ref[...], wo_ref[:nm, :]) + _dot(od_ref[...], wo_ref[nm:, :])
    x = _layer_norm(ALPHA * x_ref[...] + mix, g2_ref[...], b2_ref[...])
    o_ref[...] = _ffn_ln_rows(x, wg_ref, wu_ref, wd_ref, g3_ref, b3_ref)


def _row_tile(width):
    return pl.BlockSpec((ROW_TILE, width), lambda i: (i, 0))


def _layer_block(shape, layer):
    return pl.BlockSpec((None,) + tuple(shape), lambda *_: (layer,) + (0,) * len(shape),
                        pipeline_mode=pl.Buffered(1))


def _ffn_specs(layer):
    return [_layer_block((D_MODEL, D_FF), layer), _layer_block((D_MODEL, D_FF), layer),
            _layer_block((D_FF, D_MODEL), layer), _layer_block((1, D_MODEL), layer), _layer_block((1, D_MODEL), layer)]


def _ffn_ln(layer, x, wg, wu, wd, g, b):
    t = x.shape[0]
    return pl.pallas_call(
        _ffn_ln_kernel,
        grid=(t // ROW_TILE,),
        in_specs=[_row_tile(D_MODEL)] + _ffn_specs(layer),
        out_specs=_row_tile(D_MODEL),
        out_shape=jax.ShapeDtypeStruct((t, D_MODEL), jnp.float32),
        compiler_params=pltpu.CompilerParams(
            dimension_semantics=("parallel",), vmem_limit_bytes=VMEM_LIMIT),
        name="ffn_ln",
    )(x, wg, wu, wd, g, b)


def _mix_ffn_ln(layer, x, om, od, wo, g2, b2, wg, wu, wd, g3, b3):
    t = x.shape[0]
    nm, nd = om.shape[1], od.shape[1]
    return pl.pallas_call(
        _mix_ffn_ln_kernel,
        grid=(t // ROW_TILE,),
        in_specs=[_row_tile(D_MODEL), _row_tile(nm), _row_tile(nd), _layer_block((nm + nd, D_MODEL), layer),
                  _layer_block((1, D_MODEL), layer), _layer_block((1, D_MODEL), layer)] + _ffn_specs(layer),
        out_specs=_row_tile(D_MODEL),
        out_shape=jax.ShapeDtypeStruct((t, D_MODEL), jnp.float32),
        compiler_params=pltpu.CompilerParams(
            dimension_semantics=("parallel",), vmem_limit_bytes=VMEM_LIMIT),
        name="mix_ffn_ln",
    )(x, om, od, wo, g2, b2, wg, wu, wd, g3, b3)


def _alibi_slope(h):
    return 2.0 ** (-8.0 * (h + 1) / DIFF_HEADS)


def _inproj_kernel(x_ref, win_ref, qg_ref, wq_ref, kvg_ref, wkv_ref, rope_ref,
                   qt_ref, km_ref, vtm_ref, q1t_ref, q2t_ref, k1_ref, k2_ref, vtd_ref):
    i = pl.program_id(1)
    tm = x_ref.shape[1]
    xb = x_ref[0].astype(jnp.bfloat16)
    h = _dot(xb, win_ref[...])
    edges = np.cumsum((0,) + W_IN_SPLITS)
    c_q, c_kv, kr, dq, dk, dv = (h[:, int(a):int(b)] for a, b in zip(edges[:-1], edges[1:]))
    rope = rope_ref[...]

    q = _dot(_rms_norm(c_q, qg_ref[...]).astype(jnp.bfloat16), wq_ref[...])
    kv = _dot(_rms_norm(c_kv, kvg_ref[...]).astype(jnp.bfloat16), wkv_ref[...])
    krt = kr * rope
    k_rope = krt + pltpu.roll(krt, MLA_ROPE, axis=1)
    mla_scale = (MLA_NOPE + MLA_ROPE) ** -0.5 * LOG2E
    ones_rows = jnp.where(lax.broadcasted_iota(jnp.int32, (AUG_ROWS, tm), 0) == 0, 1.0, 0.0)
    for hh in range(MLA_HEADS):
        qh = q[:, hh * MLA_QK:(hh + 1) * MLA_QK]
        qh = jnp.concatenate([qh[:, :MLA_NOPE], qh[:, MLA_NOPE:] * rope], axis=1) * mla_scale
        qt_ref[0, hh] = qh.T.astype(jnp.bfloat16)
        kvh = kv[:, hh * MLA_KV:(hh + 1) * MLA_KV]
        km_ref[0, hh] = jnp.concatenate([kvh[:, :MLA_NOPE], k_rope], axis=1).astype(jnp.bfloat16)
        vtm_ref[0, hh, 0] = jnp.concatenate([kvh[:, MLA_NOPE:].T, ones_rows], axis=0).astype(jnp.bfloat16)

    pos_c = i * tm + lax.broadcasted_iota(jnp.int32, (AUG_ROWS, tm), 1)
    row = lax.broadcasted_iota(jnp.int32, (AUG_ROWS, tm), 0)
    hi_c = (pos_c & -POS_SPLIT).astype(jnp.float32)
    lo_c = (pos_c & (POS_SPLIT - 1)).astype(jnp.float32)
    pos_r = i * tm + lax.broadcasted_iota(jnp.int32, (tm, LANE), 0)
    lane = lax.broadcasted_iota(jnp.int32, (tm, LANE), 1)
    hi_r = (pos_r & -POS_SPLIT).astype(jnp.float32)
    lo_r = (pos_r & (POS_SPLIT - 1)).astype(jnp.float32)
    diff_scale = DIFF_DIM ** -0.5 * LOG2E
    zpad = jnp.zeros((DIFF_DIM - AUG_ROWS, tm), jnp.float32)
    def piece(slot):
        return jnp.where(slot < ALIBI_TERMS, LOG2E_PIECES[0],
                         jnp.where(slot < 2 * ALIBI_TERMS, LOG2E_PIECES[1], LOG2E_PIECES[2]))

    slot_r = lane - DIFF_DIM
    piece_c, term_c = piece(row), row & (ALIBI_TERMS - 1)
    piece_r, term_r = piece(slot_r), slot_r & (ALIBI_TERMS - 1)
    for hh in range(DIFF_HEADS):
        slope = _alibi_slope(hh)
        qaug = jnp.where(row >= ALIBI_SLOTS, 0.0,
                         jnp.where(term_c == 0, -slope * hi_c,
                                   jnp.where(term_c == 1, -slope * lo_c, piece_c)))
        kaug = jnp.where(slot_r >= ALIBI_SLOTS, 0.0,
                         jnp.where(term_r < 2, piece_r,
                                   jnp.where(term_r == 2, slope * hi_r, slope * lo_r)))
        per_head = slice(hh * 2 * DIFF_DIM, (hh + 1) * 2 * DIFF_DIM)
        tq = (dq[:, per_head] * diff_scale).T
        q1t_ref[0, hh] = jnp.concatenate([tq[:DIFF_DIM], qaug, zpad], axis=0).astype(jnp.bfloat16)
        q2t_ref[0, hh] = jnp.concatenate([tq[DIFF_DIM:], qaug, zpad], axis=0).astype(jnp.bfloat16)
        tk = dk[:, per_head]
        k1_ref[0, hh] = jnp.where(lane < DIFF_DIM, tk, kaug).astype(jnp.bfloat16)
        k2_ref[0, hh] = jnp.where(lane < DIFF_DIM, pltpu.roll(tk, DIFF_DIM, axis=1), kaug).astype(jnp.bfloat16)
        vtd_ref[0, hh, 0] = jnp.concatenate(
            [dv[:, per_head].T, ones_rows], axis=0).astype(jnp.bfloat16)


def _inproj(layer, x, win, qg, wq, kvg, wkv, rope):
    b, s, _ = x.shape
    nc = s // KV_CHUNK
    tm = ROW_TILE
    bf = jnp.bfloat16
    const = lambda shape: _layer_block(shape, layer)
    out_shape = (
        jax.ShapeDtypeStruct((b, MLA_HEADS, MLA_QK, s), bf),
        jax.ShapeDtypeStruct((b, MLA_HEADS, s, MLA_QK), bf),
        jax.ShapeDtypeStruct((b, MLA_HEADS, nc, V_ROWS, KV_CHUNK), bf),
        jax.ShapeDtypeStruct((b, DIFF_HEADS, LANE, s), bf),
        jax.ShapeDtypeStruct((b, DIFF_HEADS, LANE, s), bf),
        jax.ShapeDtypeStruct((b, DIFF_HEADS, s, LANE), bf),
        jax.ShapeDtypeStruct((b, DIFF_HEADS, s, LANE), bf),
        jax.ShapeDtypeStruct((b, DIFF_HEADS, nc, V_ROWS, KV_CHUNK), bf),
    )
    out_specs = (
        pl.BlockSpec((1, MLA_HEADS, MLA_QK, tm), lambda bb, i: (bb, 0, 0, i)),
        pl.BlockSpec((1, MLA_HEADS, tm, MLA_QK), lambda bb, i: (bb, 0, i, 0)),
        pl.BlockSpec((1, MLA_HEADS, 1, V_ROWS, tm), lambda bb, i: (bb, 0, i, 0, 0)),
        pl.BlockSpec((1, DIFF_HEADS, LANE, tm), lambda bb, i: (bb, 0, 0, i)),
        pl.BlockSpec((1, DIFF_HEADS, LANE, tm), lambda bb, i: (bb, 0, 0, i)),
        pl.BlockSpec((1, DIFF_HEADS, tm, LANE), lambda bb, i: (bb, 0, i, 0)),
        pl.BlockSpec((1, DIFF_HEADS, tm, LANE), lambda bb, i: (bb, 0, i, 0)),
        pl.BlockSpec((1, DIFF_HEADS, 1, V_ROWS, tm), lambda bb, i: (bb, 0, i, 0, 0)),
    )
    return pl.pallas_call(
        _inproj_kernel,
        grid=(b, s // tm),
        in_specs=[
            pl.BlockSpec((1, tm, D_MODEL), lambda bb, i: (bb, i, 0)),
            const((D_MODEL, W_IN_COLS)),
            const((1, MLA_Q_RANK)),
            const((MLA_Q_RANK, MLA_HEADS * MLA_QK)),
            const((1, MLA_KV_RANK)),
            const((MLA_KV_RANK, MLA_HEADS * MLA_KV)),
            pl.BlockSpec((tm, LANE), lambda bb, i: (i, 0)),
        ],
        out_specs=out_specs,
        out_shape=out_shape,
        compiler_params=pltpu.CompilerParams(
            dimension_semantics=("parallel", "parallel"), vmem_limit_bytes=VMEM_LIMIT),
        name="inproj",
    )(x, win, qg, wq, kvg, wkv, rope)


def _qk_stage(k_tile, q_tile, s_ref, mb_ref, corr=None):
    s = _dot(k_tile, q_tile)
    if corr is not None:
        s = s - corr
    s_ref[...] = s
    mb_ref[...] = jnp.max(s, axis=0, keepdims=True)


def _sm_stage(s_ref, mb_ref, p_ref, m_ref):
    m_old = m_ref[...]
    m_new = jnp.maximum(m_old, mb_ref[...])
    p_ref[...] = jnp.exp2(s_ref[...] - m_new).astype(p_ref.dtype)
    m_ref[...] = m_new
    return jnp.exp2(m_old - m_new)


def _pv_stage(vt_tiles, p_ref, acc_ref, alpha):
    acc = acc_ref[...]
    for n, vt_tile in enumerate(vt_tiles):
        acc = acc + _dot(vt_tile, p_ref[n * KV_CHUNK:(n + 1) * KV_CHUNK, :])
    if alpha is None:
        return acc
    acc_ref[...] = alpha * acc


def _key_rows(c, tk):
    return pl.ds(pl.multiple_of(c * tk, tk), tk)


def _query_cols(tile, tq):
    return pl.ds(pl.multiple_of(tile * tq, tq), tq)


def _tile_pipeline(qi, nq, nk, *, setup, qk, sm, pv, restart, finish):
    @pl.when(qi == 0)
    def _():
        setup()
        restart()
        qk(0, 0, 0)
        qk(0, 1, 1)
        sm(0)

    def step(j, slot):
        qk(qi, j + 1, 1 - slot)
        pv(j - 1, 1 - slot, sm(slot))

    def pair(t, carry):
        step(2 * t + 1, 1)
        step(2 * t + 2, 0)
        return carry

    lax.fori_loop(0, nk // 2 - 1, pair, 0)

    @pl.when(qi < nq - 1)
    def _():
        qk(qi + 1, 0, 0)
        pv(nk - 2, 0, sm(1))
        qk(qi + 1, 1, 1)
        restart()
        sm(0)
        finish(pv(nk - 1, 1, None))

    @pl.when(qi == nq - 1)
    def _():
        pv(nk - 2, 0, sm(1))
        finish(pv(nk - 1, 1, None))


def _mla_kernel(qt_ref, k_ref, vt_ref, o_ref, s0, s1, p0, p1, mb0, mb1, m_sc, acc_sc):
    qi = pl.program_id(2)
    tk, tq = s0.shape
    sub = tk // KV_CHUNK
    nk = vt_ref.shape[2] // sub
    s_bufs, p_bufs, mb_bufs = (s0, s1), (p0, p1), (mb0, mb1)
    halves = [pl.ds(i * (tq // 2), tq // 2) for i in range(2)]

    def setup():
        acc_sc[...] = jnp.zeros_like(acc_sc)

    def qk(tile, j, slot):
        for i, hv in enumerate(halves):
            cols = pl.ds(pl.multiple_of(tile * tq + i * (tq // 2), tq // 2), tq // 2)
            _qk_stage(k_ref[0, 0, _key_rows(j, tk), :], qt_ref[0, 0, :, cols],
                      s_bufs[slot].at[:, hv], mb_bufs[slot].at[:, hv])

    def sm(slot):
        return [_sm_stage(s_bufs[slot].at[:, hv], mb_bufs[slot].at[:, hv], p_bufs[slot].at[:, hv],
                          m_sc.at[:, hv]) for hv in halves]

    def pv(j, slot, alphas):
        vt = [vt_ref[0, 0, j * sub + n] for n in range(sub)]
        return [_pv_stage(vt, p_bufs[slot].at[:, hv], acc_sc.at[:, hv], None if alphas is None else alphas[i])
                for i, hv in enumerate(halves)]

    def restart():
        m_sc[...] = jnp.full_like(m_sc, NEG_INF)

    def finish(totals):
        acc = jnp.concatenate(totals, axis=1)
        o = acc[:MLA_V] / acc[MLA_V:MLA_V + 1]
        o_ref[0] = o.T.astype(o_ref.dtype)
        acc_sc[...] = jnp.zeros_like(acc_sc)

    _tile_pipeline(qi, qt_ref.shape[3] // tq, nk, setup=setup, qk=qk, sm=sm, pv=pv, restart=restart, finish=finish)


def _mla_attention(qt, k, vt):
    b, hds, _, s = qt.shape
    nc = vt.shape[2]
    tq, tk = MLA_Q_TILE, MLA_K_TILE
    assert tk % KV_CHUNK == 0 and s % (2 * tk) == 0 and s % tq == 0
    f32, bf = jnp.float32, jnp.bfloat16
    return pl.pallas_call(
        _mla_kernel,
        grid=(b, hds, s // tq),
        in_specs=[
            pl.BlockSpec((1, 1, MLA_QK, s), lambda bb, h, i: (bb, h, 0, 0)),
            pl.BlockSpec((1, 1, s, MLA_QK), lambda bb, h, i: (bb, h, 0, 0)),
            pl.BlockSpec((1, 1, nc, V_ROWS, KV_CHUNK), lambda bb, h, i: (bb, h, 0, 0, 0)),
        ],
        out_specs=pl.BlockSpec((1, tq, MLA_V), lambda bb, h, i: (bb, i, h)),
        out_shape=jax.ShapeDtypeStruct((b, s, hds * MLA_V), bf),
        scratch_shapes=[
            pltpu.VMEM((tk, tq), f32), pltpu.VMEM((tk, tq), f32),
            pltpu.VMEM((tk, tq), bf), pltpu.VMEM((tk, tq), bf),
            pltpu.VMEM((1, tq), f32), pltpu.VMEM((1, tq), f32),
            pltpu.VMEM((1, tq), f32),
            pltpu.VMEM((V_ROWS, tq), f32),
        ],
        compiler_params=pltpu.CompilerParams(
            dimension_semantics=("parallel", "parallel", "arbitrary"), vmem_limit_bytes=VMEM_LIMIT),
        name="mla_attn",
    )(qt, k, vt)


def _diff_kernel(q1t_ref, q2t_ref, k1_ref, k2_ref, vt_ref, lam_ref, linit_ref, g_ref, o_ref,
                 qv_sc, corr_sc, s_sc, p_sc, mb_sc, m_sc, acc_sc):
    h = pl.program_id(1)
    qi = pl.program_id(2)
    tq = o_ref.shape[1]
    sub = tq // KV_CHUNK
    nk = vt_ref.shape[2] // sub
    k_refs = (k1_ref, k2_ref)
    q_refs = (q1t_ref, q2t_ref)

    def setup():
        jj = lax.broadcasted_iota(jnp.int32, (tq, tq), 0)
        ii = lax.broadcasted_iota(jnp.int32, (tq, tq), 1)
        expo = jnp.full((1, tq), 127 + 1, jnp.int32) - 2 * (h + 1)
        slope2 = lax.bitcast_convert_type(expo << 23, jnp.float32)
        corr_sc[...] = (slope2 * LOG2E) * jnp.maximum(jj - ii, 0).astype(jnp.float32)
        acc_sc[...] = jnp.zeros_like(acc_sc)

    def chunk(tile, j):
        return lax.rem(tile + j, nk)

    def load_queries(tile):
        row = lax.broadcasted_iota(jnp.int32, (LANE, tq), 0)
        for mi in range(2):
            qq = q_refs[mi][0, 0, :, _query_cols(tile, tq)]
            qv_sc[mi] = qq
            qv_sc[2 + mi] = jnp.where(row < DIFF_DIM, qq, -qq)

    def qk(tile, j, slot):
        diagonal = isinstance(j, int) and j == 0
        if diagonal:
            load_queries(tile)
        c = chunk(tile, j)
        variant = 0 if diagonal else jnp.where(c > tile, 2, 0)
        for mi in range(2):
            _qk_stage(k_refs[mi][0, 0, _key_rows(c, tq), :], qv_sc[variant + mi],
                      s_sc.at[2 * slot + mi], mb_sc.at[2 * slot + mi],
                      corr_sc[...] if diagonal else None)

    def sm(slot):
        return [_sm_stage(s_sc.at[2 * slot + mi], mb_sc.at[2 * slot + mi], p_sc.at[2 * slot + mi],
                          m_sc.at[mi]) for mi in range(2)]

    def pv(j, slot, alphas):
        c = chunk(qi, j)
        vt = [vt_ref[0, 0, c * sub + n] for n in range(sub)]
        return [_pv_stage(vt, p_sc.at[2 * slot + mi], acc_sc.at[mi], None if alphas is None else alphas[mi])
                for mi in range(2)]

    def restart():
        m_sc[...] = jnp.full_like(m_sc, NEG_INF)

    def finish(totals):
        lamp = lam_ref[...]
        linit = linit_ref[...]
        lam = (jnp.exp(jnp.sum(lamp[0:1] * lamp[1:2], axis=-1, keepdims=True))
               - jnp.exp(jnp.sum(lamp[2:3] * lamp[3:4], axis=-1, keepdims=True)) + linit)
        nv = 2 * DIFF_DIM
        a1, a2 = totals
        o = a1[:nv] * (1.0 / a1[nv:nv + 1]) - a2[:nv] * (lam / a2[nv:nv + 1])
        o = o * lax.rsqrt(jnp.mean(o * o, axis=0, keepdims=True) + NORM_EPS)
        o_ref[0] = (o.T * (g_ref[...] * (1.0 - linit))).astype(o_ref.dtype)
        acc_sc[...] = jnp.zeros_like(acc_sc)

    _tile_pipeline(qi, q1t_ref.shape[3] // tq, nk, setup=setup, qk=qk, sm=sm, pv=pv, restart=restart, finish=finish)


def _diff_attention(layer, q1t, q2t, k1, k2, vt, lamp, linit, subln_g):
    b, hds, _, s = q1t.shape
    nc = vt.shape[2]
    t = DIFF_TILE
    assert t % KV_CHUNK == 0 and s % (2 * t) == 0
    f32, bf = jnp.float32, jnp.bfloat16
    qspec = pl.BlockSpec((1, 1, LANE, s), lambda bb, h, i: (bb, h, 0, 0))
    kspec = pl.BlockSpec((1, 1, s, LANE), lambda bb, h, i: (bb, h, 0, 0))
    const = lambda shape: _layer_block(shape, layer)
    return pl.pallas_call(
        _diff_kernel,
        grid=(b, hds, s // t),
        in_specs=[
            qspec, qspec, kspec, kspec,
            pl.BlockSpec((1, 1, nc, V_ROWS, KV_CHUNK), lambda bb, h, i: (bb, h, 0, 0, 0)),
            const((4, DIFF_DIM)), const((1, 1)), const((1, 2 * DIFF_DIM)),
        ],
        out_specs=pl.BlockSpec((1, t, 2 * DIFF_DIM), lambda bb, h, i: (bb, i, h)),
        out_shape=jax.ShapeDtypeStruct((b, s, hds * 2 * DIFF_DIM), bf),
        scratch_shapes=[
            pltpu.VMEM((4, LANE, t), bf),
            pltpu.VMEM((t, t), f32),
            pltpu.VMEM((4, t, t), f32),
            pltpu.VMEM((4, t, t), bf),
            pltpu.VMEM((4, 1, t), f32),
            pltpu.VMEM((2, 1, t), f32),
            pltpu.VMEM((2, V_ROWS, t), f32),
        ],
        compiler_params=pltpu.CompilerParams(
            dimension_semantics=("parallel", "parallel", "arbitrary"), vmem_limit_bytes=VMEM_LIMIT),
        name="diff_attn",
    )(q1t, q2t, k1, k2, vt, lamp, linit, subln_g)


def _swap_halves(w):
    half = w.shape[-1] // 2
    return jnp.concatenate([w[..., half:], w[..., :half]], axis=-1)


def _prep_w_in_kernel(w_ref, o_ref):
    w = w_ref[...]
    rope_end = MLA_Q_RANK + MLA_KV_RANK + MLA_ROPE
    k_rope = w[:, rope_end - MLA_ROPE:rope_end]
    o_ref[...] = jnp.concatenate(
        [w[:, :rope_end], _swap_halves(k_rope), w[:, rope_end:]], axis=-1).astype(o_ref.dtype)


def _prep_w_in(w_in):
    l, d, n = w_in.shape
    rows = ROW_TILE // 2
    return pl.pallas_call(
        _prep_w_in_kernel,
        grid=(l, d // rows),
        in_specs=[pl.BlockSpec((None, rows, n), lambda a, i: (a, i, 0))],
        out_specs=pl.BlockSpec((None, rows, W_IN_COLS), lambda a, i: (a, i, 0)),
        out_shape=jax.ShapeDtypeStruct((l, d, W_IN_COLS), jnp.bfloat16),
        compiler_params=pltpu.CompilerParams(dimension_semantics=("parallel", "parallel")),
        name="prep_w_in",
    )(w_in)


def _prep_w_q_up(w_q_up):
    l, r, _ = w_q_up.shape
    w = w_q_up.astype(jnp.bfloat16).reshape(l, r, MLA_HEADS, MLA_NOPE + MLA_ROPE)
    w = jnp.concatenate([w, _swap_halves(w[..., MLA_NOPE:])], axis=-1)
    return w.reshape(l, r, MLA_HEADS * MLA_QK)


def _rope_table(s):
    inv_freq = np.float32(ROPE_THETA) ** (-np.arange(0, MLA_ROPE, 2, dtype=np.float32) / np.float32(MLA_ROPE))
    ang = np.arange(s, dtype=np.float32)[:, None] * inv_freq[None, :]
    cos, sin = np.cos(ang), np.sin(ang)
    return jnp.asarray(np.concatenate([cos, cos, -sin, sin], axis=-1), jnp.float32)


def kernel(x, ffn1_w_gate, ffn1_w_up, ffn1_w_down, ln1_g, ln1_b, w_in, q_norm_g, w_q_up, kv_norm_g, w_kv_up, diff_lambda_q1, diff_lambda_k1, diff_lambda_q2, diff_lambda_k2, diff_subln_g, w_out, ln2_g, ln2_b, ffn2_w_gate, ffn2_w_up, ffn2_w_down, ln3_g, ln3_b):
    b, s, d = x.shape
    bf = jnp.bfloat16
    rope = _rope_table(s)
    w_in_p = _prep_w_in(w_in)
    w_q_p = _prep_w_q_up(w_q_up)
    w_kv_p = w_kv_up.astype(bf)
    w_out_p = w_out.astype(bf)
    f1 = (ffn1_w_gate.astype(bf), ffn1_w_up.astype(bf), ffn1_w_down.astype(bf))
    f2 = (ffn2_w_gate.astype(bf), ffn2_w_up.astype(bf), ffn2_w_down.astype(bf))
    lamp = jnp.stack([diff_lambda_q1, diff_lambda_k1, diff_lambda_q2, diff_lambda_k2], axis=1)
    linit = jnp.asarray([[[0.8 - 0.6 * math.exp(-0.3 * l)]] for l in range(DEPTH)], jnp.float32)
    rows = lambda v: v[:, None, :]

    xf = x.reshape(b * s, d)
    for l in range(DEPTH):
        xf = _ffn_ln(l, xf, *f1, rows(ln1_g), rows(ln1_b))
        qt, km, vtm, q1t, q2t, k1, k2, vtd = _inproj(
            l, xf.reshape(b, s, d), w_in_p, rows(q_norm_g), w_q_p, rows(kv_norm_g), w_kv_p, rope)
        o_mla = _mla_attention(qt, km, vtm)
        o_diff = _diff_attention(l, q1t, q2t, k1, k2, vtd, lamp, linit, rows(diff_subln_g))
        xf = _mix_ffn_ln(l, xf, o_mla.reshape(b * s, -1), o_diff.reshape(b * s, -1), w_out_p,
                         rows(ln2_g), rows(ln2_b), *f2, rows(ln3_g), rows(ln3_b))
    return xf.reshape(b, s, d)
```

```python
import math

import jax
import jax.numpy as jnp
import numpy as np
from jax import lax
from jax.experimental import pallas as pl
from jax.experimental.pallas import tpu as pltpu

D_MODEL = 1024
DEPTH = 4
MLA_HEADS = 4
MLA_NOPE = 128
MLA_ROPE = 64
MLA_V = 128
MLA_Q_RANK = 256
MLA_KV_RANK = 128
DIFF_HEADS = 4
DIFF_DIM = 64
D_FF = 2816
ROPE_THETA = 10000.0
NORM_EPS = 1e-5
ALPHA = (2 * DEPTH) ** 0.25

LANE = 128
ROW_TILE = 512
KV_CHUNK = ROW_TILE
DIFF_TILE = 1024
MLA_Q_TILE = 2048
MLA_K_TILE = 1024
POS_SPLIT = 128
VMEM_V7X = 64 * 1024 * 1024
VMEM_LIMIT = VMEM_V7X - 6 * 1024 * 1024

MLA_QK = MLA_NOPE + 2 * MLA_ROPE
MLA_KV = MLA_NOPE + MLA_V
DIFF_W = DIFF_HEADS * 2 * DIFF_DIM
W_IN_SPLITS = (MLA_Q_RANK, MLA_KV_RANK, 2 * MLA_ROPE, DIFF_W, DIFF_W, DIFF_W)
W_IN_COLS = sum(W_IN_SPLITS)
NEG_INF = float("-inf")
LOG2E = math.log2(math.e)
AUG_ROWS = 16
V_ROWS = MLA_V + AUG_ROWS


def _bf16_pieces(x, n):
    out = []
    for _ in range(n):
        piece = float(np.asarray(x, dtype=jnp.bfloat16).astype(np.float32))
        out.append(piece)
        x -= piece
    return tuple(out)


LOG2E_PIECES = _bf16_pieces(LOG2E, 3)
ALIBI_TERMS = 4
ALIBI_SLOTS = len(LOG2E_PIECES) * ALIBI_TERMS
assert ALIBI_SLOTS <= AUG_ROWS


def _dot(a, b):
    return jnp.dot(a, b, preferred_element_type=jnp.float32)


def _layer_norm(y, g, b):
    mu = jnp.mean(y, axis=-1, keepdims=True)
    d = y - mu
    var = jnp.mean(d * d, axis=-1, keepdims=True)
    return d * lax.rsqrt(var + NORM_EPS) * g + b


def _rms_norm(y, g):
    return y * lax.rsqrt(jnp.mean(y * y, axis=-1, keepdims=True) + NORM_EPS) * g


def _ffn_ln_rows(x, wg_ref, wu_ref, wd_ref, g_ref, b_ref):
    xb = x.astype(jnp.bfloat16)
    gate = _dot(xb, wg_ref[...])
    up = _dot(xb, wu_ref[...])
    h = gate / (1.0 + jnp.exp(-gate)) * up
    ffn = _dot(h.astype(jnp.bfloat16), wd_ref[...])
    return _layer_norm(ALPHA * x + 0.5 * ffn, g_ref[...], b_ref[...])


def _row_halves(n):
    return [slice(0, n // 2), slice(n // 2, n)]


def _ffn_ln_kernel(x_ref, wg_ref, wu_ref, wd_ref, g_ref, b_ref, o_ref):
    for rows in _row_halves(x_ref.shape[0]):
        o_ref[rows, :] = _ffn_ln_rows(x_ref[rows, :], wg_ref, wu_ref, wd_ref, g_ref, b_ref)


def _mix_ffn_ln_kernel(x_ref, om_ref, od_ref, wo_ref, g2_ref, b2_ref,
                       wg_ref, wu_ref, wd_ref, g3_ref, b3_ref, o_ref):
    nm = om_ref.shape[1]
    mix = _dot(om_ref[...], wo_ref[:nm, :]) + _dot(od_ref[...], wo_ref[nm:, :])
    x = _layer_norm(ALPHA * x_ref[...] + mix, g2_ref[...], b2_ref[...])
    o_ref[...] = _ffn_ln_rows(x, wg_ref, wu_ref, wd_ref, g3_ref, b3_ref)


def _row_tile(width):
    return pl.BlockSpec((ROW_TILE, width), lambda i: (i, 0))


def _layer_block(shape, layer):
    return pl.BlockSpec((None,) + tuple(shape), lambda *_: (layer,) + (0,) * len(shape),
                        pipeline_mode=pl.Buffered(1))


def _ffn_specs(layer):
    return [_layer_block((D_MODEL, D_FF), layer), _layer_block((D_MODEL, D_FF), layer),
            _layer_block((D_FF, D_MODEL), layer), _layer_block((1, D_MODEL), layer), _layer_block((1, D_MODEL), layer)]


def _ffn_ln(layer, x, wg, wu, wd, g, b):
    t = x.shape[0]
    return pl.pallas_call(
        _ffn_ln_kernel,
        grid=(t // ROW_TILE,),
        in_specs=[_row_tile(D_MODEL)] + _ffn_specs(layer),
        out_specs=_row_tile(D_MODEL),
        out_shape=jax.ShapeDtypeStruct((t, D_MODEL), jnp.float32),
        compiler_params=pltpu.CompilerParams(
            dimension_semantics=("parallel",), vmem_limit_bytes=VMEM_LIMIT),
        name="ffn_ln",
    )(x, wg, wu, wd, g, b)


def _mix_ffn_ln(layer, x, om, od, wo, g2, b2, wg, wu, wd, g3, b3):
    t = x.shape[0]
    nm, nd = om.shape[1], od.shape[1]
    return pl.pallas_call(
        _mix_ffn_ln_kernel,
        grid=(t // ROW_TILE,),
        in_specs=[_row_tile(D_MODEL), _row_tile(nm), _row_tile(nd), _layer_block((nm + nd, D_MODEL), layer),
                  _layer_block((1, D_MODEL), layer), _layer_block((1, D_MODEL), layer)] + _ffn_specs(layer),
        out_specs=_row_tile(D_MODEL),
        out_shape=jax.ShapeDtypeStruct((t, D_MODEL), jnp.float32),
        compiler_params=pltpu.CompilerParams(
            dimension_semantics=("parallel",), vmem_limit_bytes=VMEM_LIMIT),
        name="mix_ffn_ln",
    )(x, om, od, wo, g2, b2, wg, wu, wd, g3, b3)


def _alibi_slope(h):
    return 2.0 ** (-8.0 * (h + 1) / DIFF_HEADS)


def _inproj_kernel(x_ref, win_ref, qg_ref, wq_ref, kvg_ref, wkv_ref, rope_ref,
                   qt_ref, km_ref, vtm_ref, q1t_ref, q2t_ref, k1_ref, k2_ref, vtd_ref):
    i = pl.program_id(1)
    tm = x_ref.shape[1]
    xb = x_ref[0].astype(jnp.bfloat16)
    h = _dot(xb, win_ref[...])
    edges = np.cumsum((0,) + W_IN_SPLITS)
    c_q, c_kv, kr, dq, dk, dv = (h[:, int(a):int(b)] for a, b in zip(edges[:-1], edges[1:]))
    rope = rope_ref[...]

    q = _dot(_rms_norm(c_q, qg_ref[...]).astype(jnp.bfloat16), wq_ref[...])
    kv = _dot(_rms_norm(c_kv, kvg_ref[...]).astype(jnp.bfloat16), wkv_ref[...])
    krt = kr * rope
    k_rope = krt + pltpu.roll(krt, MLA_ROPE, axis=1)
    mla_scale = (MLA_NOPE + MLA_ROPE) ** -0.5 * LOG2E
    ones_rows = jnp.where(lax.broadcasted_iota(jnp.int32, (AUG_ROWS, tm), 0) == 0, 1.0, 0.0)
    for hh in range(MLA_HEADS):
        qh = q[:, hh * MLA_QK:(hh + 1) * MLA_QK]
        qh = jnp.concatenate([qh[:, :MLA_NOPE], qh[:, MLA_NOPE:] * rope], axis=1) * mla_scale
        qt_ref[0, hh] = qh.T.astype(jnp.bfloat16)
        kvh = kv[:, hh * MLA_KV:(hh + 1) * MLA_KV]
        km_ref[0, hh] = jnp.concatenate([kvh[:, :MLA_NOPE], k_rope], axis=1).astype(jnp.bfloat16)
        vtm_ref[0, hh, 0] = jnp.concatenate([kvh[:, MLA_NOPE:].T, ones_rows], axis=0).astype(jnp.bfloat16)

    pos_c = i * tm + lax.broadcasted_iota(jnp.int32, (AUG_ROWS, tm), 1)
    row = lax.broadcasted_iota(jnp.int32, (AUG_ROWS, tm), 0)
    hi_c = (pos_c & -POS_SPLIT).astype(jnp.float32)
    lo_c = (pos_c & (POS_SPLIT - 1)).astype(jnp.float32)
    pos_r = i * tm + lax.broadcasted_iota(jnp.int32, (tm, LANE), 0)
    lane = lax.broadcasted_iota(jnp.int32, (tm, LANE), 1)
    hi_r = (pos_r & -POS_SPLIT).astype(jnp.float32)
    lo_r = (pos_r & (POS_SPLIT - 1)).astype(jnp.float32)
    diff_scale = DIFF_DIM ** -0.5 * LOG2E
    zpad = jnp.zeros((DIFF_DIM - AUG_ROWS, tm), jnp.float32)
    def piece(slot):
        return jnp.where(slot < ALIBI_TERMS, LOG2E_PIECES[0],
                         jnp.where(slot < 2 * ALIBI_TERMS, LOG2E_PIECES[1], LOG2E_PIECES[2]))

    slot_r = lane - DIFF_DIM
    piece_c, term_c = piece(row), row & (ALIBI_TERMS - 1)
    piece_r, term_r = piece(slot_r), slot_r & (ALIBI_TERMS - 1)
    for hh in range(DIFF_HEADS):
        slope = _alibi_slope(hh)
        qaug = jnp.where(row >= ALIBI_SLOTS, 0.0,
                         jnp.where(term_c == 0, -slope * hi_c,
                                   jnp.where(term_c == 1, -slope * lo_c, piece_c)))
        kaug = jnp.where(slot_r >= ALIBI_SLOTS, 0.0,
                         jnp.where(term_r < 2, piece_r,
                                   jnp.where(term_r == 2, slope * hi_r, slope * lo_r)))
        per_head = slice(hh * 2 * DIFF_DIM, (hh + 1) * 2 * DIFF_DIM)
        tq = (dq[:, per_head] * diff_scale).T
        q1t_ref[0, hh] = jnp.concatenate([tq[:DIFF_DIM], qaug, zpad], axis=0).astype(jnp.bfloat16)
        q2t_ref[0, hh] = jnp.concatenate([tq[DIFF_DIM:], qaug, zpad], axis=0).astype(jnp.bfloat16)
        tk = dk[:, per_head]
        k1_ref[0, hh] = jnp.where(lane < DIFF_DIM, tk, kaug).astype(jnp.bfloat16)
        k2_ref[0, hh] = jnp.where(lane < DIFF_DIM, pltpu.roll(tk, DIFF_DIM, axis=1), kaug).astype(jnp.bfloat16)
        vtd_ref[0, hh, 0] = jnp.concatenate(
            [dv[:, per_head].T, ones_rows], axis=0).astype(jnp.bfloat16)


def _inproj(layer, x, win, qg, wq, kvg, wkv, rope):
    b, s, _ = x.shape
    nc = s // KV_CHUNK
    tm = ROW_TILE
    bf = jnp.bfloat16
    const = lambda shape: _layer_block(shape, layer)
    out_shape = (
        jax.ShapeDtypeStruct((b, MLA_HEADS, MLA_QK, s), bf),
        jax.ShapeDtypeStruct((b, MLA_HEADS, s, MLA_QK), bf),
        jax.ShapeDtypeStruct((b, MLA_HEADS, nc, V_ROWS, KV_CHUNK), bf),
        jax.ShapeDtypeStruct((b, DIFF_HEADS, LANE, s), bf),
        jax.ShapeDtypeStruct((b, DIFF_HEADS, LANE, s), bf),
        jax.ShapeDtypeStruct((b, DIFF_HEADS, s, LANE), bf),
        jax.ShapeDtypeStruct((b, DIFF_HEADS, s, LANE), bf),
        jax.ShapeDtypeStruct((b, DIFF_HEADS, nc, V_ROWS, KV_CHUNK), bf),
    )
    out_specs = (
        pl.BlockSpec((1, MLA_HEADS, MLA_QK, tm), lambda bb, i: (bb, 0, 0, i)),
        pl.BlockSpec((1, MLA_HEADS, tm, MLA_QK), lambda bb, i: (bb, 0, i, 0)),
        pl.BlockSpec((1, MLA_HEADS, 1, V_ROWS, tm), lambda bb, i: (bb, 0, i, 0, 0)),
        pl.BlockSpec((1, DIFF_HEADS, LANE, tm), lambda bb, i: (bb, 0, 0, i)),
        pl.BlockSpec((1, DIFF_HEADS, LANE, tm), lambda bb, i: (bb, 0, 0, i)),
        pl.BlockSpec((1, DIFF_HEADS, tm, LANE), lambda bb, i: (bb, 0, i, 0)),
        pl.BlockSpec((1, DIFF_HEADS, tm, LANE), lambda bb, i: (bb, 0, i, 0)),
        pl.BlockSpec((1, DIFF_HEADS, 1, V_ROWS, tm), lambda bb, i: (bb, 0, i, 0, 0)),
    )
    return pl.pallas_call(
        _inproj_kernel,
        grid=(b, s // tm),
        in_specs=[
            pl.BlockSpec((1, tm, D_MODEL), lambda bb, i: (bb, i, 0)),
            const((D_MODEL, W_IN_COLS)),
            const((1, MLA_Q_RANK)),
            const((MLA_Q_RANK, MLA_HEADS * MLA_QK)),
            const((1, MLA_KV_RANK)),
            const((MLA_KV_RANK, MLA_HEADS * MLA_KV)),
            pl.BlockSpec((tm, LANE), lambda bb, i: (i, 0)),
        ],
        out_specs=out_specs,
        out_shape=out_shape,
        compiler_params=pltpu.CompilerParams(
            dimension_semantics=("parallel", "parallel"), vmem_limit_bytes=VMEM_LIMIT),
        name="inproj",
    )(x, win, qg, wq, kvg, wkv, rope)


def _qk_stage(k_tile, q_tile, s_ref, mb_ref, corr=None):
    s = _dot(k_tile, q_tile)
    if corr is not None:
        s = s - corr
    s_ref[...] = s
    mb_ref[...] = jnp.max(s, axis=0, keepdims=True)


def _sm_stage(s_ref, mb_ref, p_ref, m_ref):
    m_old = m_ref[...]
    m_new = jnp.maximum(m_old, mb_ref[...])
    p_ref[...] = jnp.exp2(s_ref[...] - m_new).astype(p_ref.dtype)
    m_ref[...] = m_new
    return jnp.exp2(m_old - m_new)


def _pv_stage(vt_tiles, p_ref, acc_ref, alpha):
    acc = acc_ref[...]
    for n, vt_tile in enumerate(vt_tiles):
        acc = acc + _dot(vt_tile, p_ref[n * KV_CHUNK:(n + 1) * KV_CHUNK, :])
    if alpha is None:
        return acc
    acc_ref[...] = alpha * acc


def _key_rows(c, tk):
    return pl.ds(pl.multiple_of(c * tk, tk), tk)


def _query_cols(tile, tq):
    return pl.ds(pl.multiple_of(tile * tq, tq), tq)


def _tile_pipeline(qi, nq, nk, first_head, last_head, *, setup, setup_next, qk, sm, pv, restart, finish):
    last_tile = qi == nq - 1

    @pl.when(jnp.logical_and(first_head, qi == 0))
    def _():
        setup()
        restart()
        qk(0, 0, 0, False)
        qk(0, 1, 1, False)
        sm(0)

    def step(j, slot):
        qk(qi, j + 1, 1 - slot, False)
        pv(j - 1, 1 - slot, sm(slot))

    def pair(t, carry):
        step(2 * t + 1, 1)
        step(2 * t + 2, 0)
        return carry

    lax.fori_loop(0, nk // 2 - 1, pair, 0)

    def hand_over(tile, nxt):
        qk(tile, 0, 0, nxt)
        pv(nk - 2, 0, sm(1))
        qk(tile, 1, 1, nxt)
        restart()
        sm(0)
        finish(pv(nk - 1, 1, None))

    @pl.when(jnp.logical_not(last_tile))
    def _():
        hand_over(qi + 1, False)

    @pl.when(jnp.logical_and(last_tile, jnp.logical_not(last_head)))
    def _():
        setup_next()
        hand_over(0, True)

    @pl.when(jnp.logical_and(last_tile, last_head))
    def _():
        pv(nk - 2, 0, sm(1))
        finish(pv(nk - 1, 1, None))


def _head_flags():
    b, h = pl.program_id(0), pl.program_id(1)
    first = jnp.logical_and(b == 0, h == 0)
    last = jnp.logical_and(b == pl.num_programs(0) - 1, h == pl.num_programs(1) - 1)
    return first, last


def _next_head(bb, h, n_heads, n_batch):
    flat = jnp.minimum(bb * n_heads + h + 1, n_batch * n_heads - 1)
    return flat // n_heads, flat % n_heads


def _mla_kernel(qt_ref, k_ref, vt_ref, qt_nx_ref, k_nx_ref, o_ref, s0, s1, p0, p1, mb0, mb1, m_sc, acc_sc):
    qi = pl.program_id(2)
    tk, tq = s0.shape
    sub = tk // KV_CHUNK
    nk = vt_ref.shape[2] // sub
    s_bufs, p_bufs, mb_bufs = (s0, s1), (p0, p1), (mb0, mb1)
    halves = [pl.ds(i * (tq // 2), tq // 2) for i in range(2)]

    def setup():
        acc_sc[...] = jnp.zeros_like(acc_sc)

    def qk(tile, j, slot, nxt):
        q_src, k_src = (qt_nx_ref, k_nx_ref) if nxt else (qt_ref, k_ref)
        for i, hv in enumerate(halves):
            cols = pl.ds(pl.multiple_of(tile * tq + i * (tq // 2), tq // 2), tq // 2)
            _qk_stage(k_src[0, 0, _key_rows(j, tk), :], q_src[0, 0, :, cols],
                      s_bufs[slot].at[:, hv], mb_bufs[slot].at[:, hv])

    def sm(slot):
        return [_sm_stage(s_bufs[slot].at[:, hv], mb_bufs[slot].at[:, hv], p_bufs[slot].at[:, hv],
                          m_sc.at[:, hv]) for hv in halves]

    def pv(j, slot, alphas):
        vt = [vt_ref[0, 0, j * sub + n] for n in range(sub)]
        return [_pv_stage(vt, p_bufs[slot].at[:, hv], acc_sc.at[:, hv], None if alphas is None else alphas[i])
                for i, hv in enumerate(halves)]

    def restart():
        m_sc[...] = jnp.full_like(m_sc, NEG_INF)

    def finish(totals):
        acc = jnp.concatenate(totals, axis=1)
        o = acc[:MLA_V] / acc[MLA_V:MLA_V + 1]
        o_ref[0] = o.T.astype(o_ref.dtype)
        acc_sc[...] = jnp.zeros_like(acc_sc)

    _tile_pipeline(qi, qt_ref.shape[3] // tq, nk, *_head_flags(), setup=setup, setup_next=lambda: None,
                   qk=qk, sm=sm, pv=pv, restart=restart, finish=finish)


def _mla_attention(qt, k, vt):
    b, hds, _, s = qt.shape
    nc = vt.shape[2]
    tq, tk = MLA_Q_TILE, MLA_K_TILE
    assert tk % KV_CHUNK == 0 and s % (2 * tk) == 0 and s % tq == 0
    f32, bf = jnp.float32, jnp.bfloat16
    return pl.pallas_call(
        _mla_kernel,
        grid=(b, hds, s // tq),
        in_specs=[
            pl.BlockSpec((1, 1, MLA_QK, s), lambda bb, h, i: (bb, h, 0, 0)),
            pl.BlockSpec((1, 1, s, MLA_QK), lambda bb, h, i: (bb, h, 0, 0)),
            pl.BlockSpec((1, 1, nc, V_ROWS, KV_CHUNK), lambda bb, h, i: (bb, h, 0, 0, 0)),
            pl.BlockSpec((1, 1, MLA_QK, tq), lambda bb, h, i: (*_next_head(bb, h, hds, b), 0, 0)),
            pl.BlockSpec((1, 1, 2 * tk, MLA_QK), lambda bb, h, i: (*_next_head(bb, h, hds, b), 0, 0)),
        ],
        out_specs=pl.BlockSpec((1, tq, MLA_V), lambda bb, h, i: (bb, i, h)),
        out_shape=jax.ShapeDtypeStruct((b, s, hds * MLA_V), bf),
        scratch_shapes=[
            pltpu.VMEM((tk, tq), f32), pltpu.VMEM((tk, tq), f32),
            pltpu.VMEM((tk, tq), bf), pltpu.VMEM((tk, tq), bf),
            pltpu.VMEM((1, tq), f32), pltpu.VMEM((1, tq), f32),
            pltpu.VMEM((1, tq), f32),
            pltpu.VMEM((V_ROWS, tq), f32),
        ],
        compiler_params=pltpu.CompilerParams(
            dimension_semantics=("arbitrary", "arbitrary", "arbitrary"), vmem_limit_bytes=VMEM_LIMIT),
        name="mla_attn",
    )(qt, k, vt, qt, k)


def _diff_kernel(q1t_ref, q2t_ref, k1_ref, k2_ref, vt_ref, q1n_ref, q2n_ref, k1n_ref, k2n_ref,
                 lam_ref, linit_ref, g_ref, o_ref,
                 qv_sc, corr_sc, s_sc, p_sc, mb_sc, m_sc, acc_sc):
    h = pl.program_id(1)
    qi = pl.program_id(2)
    tq = o_ref.shape[1]
    sub = tq // KV_CHUNK
    nk = vt_ref.shape[2] // sub
    k_refs = (k1_ref, k2_ref)
    q_refs = (q1t_ref, q2t_ref)
    k_next = (k1n_ref, k2n_ref)
    q_next = (q1n_ref, q2n_ref)

    def build_corr(head):
        jj = lax.broadcasted_iota(jnp.int32, (tq, tq), 0)
        ii = lax.broadcasted_iota(jnp.int32, (tq, tq), 1)
        expo = jnp.full((1, tq), 127 + 1, jnp.int32) - 2 * (head + 1)
        slope2 = lax.bitcast_convert_type(expo << 23, jnp.float32)
        corr_sc[...] = (slope2 * LOG2E) * jnp.maximum(jj - ii, 0).astype(jnp.float32)

    def setup():
        build_corr(h)
        acc_sc[...] = jnp.zeros_like(acc_sc)

    def setup_next():
        build_corr(lax.rem(h + 1, pl.num_programs(1)))

    def chunk(tile, j):
        return lax.rem(tile + j, nk)

    def load_queries(tile, nxt):
        row = lax.broadcasted_iota(jnp.int32, (LANE, tq), 0)
        for mi in range(2):
            qq = (q_next if nxt else q_refs)[mi][0, 0, :, _query_cols(tile, tq)]
            qv_sc[mi] = qq
            qv_sc[2 + mi] = jnp.where(row < DIFF_DIM, qq, -qq)

    def qk(tile, j, slot, nxt):
        diagonal = isinstance(j, int) and j == 0
        if diagonal:
            load_queries(tile, nxt)
        c = chunk(tile, j)
        variant = 0 if diagonal else jnp.where(c > tile, 2, 0)
        for mi in range(2):
            _qk_stage((k_next if nxt else k_refs)[mi][0, 0, _key_rows(c, tq), :], qv_sc[variant + mi],
                      s_sc.at[2 * slot + mi], mb_sc.at[2 * slot + mi],
                      corr_sc[...] if diagonal else None)

    def sm(slot):
        return [_sm_stage(s_sc.at[2 * slot + mi], mb_sc.at[2 * slot + mi], p_sc.at[2 * slot + mi],
                          m_sc.at[mi]) for mi in range(2)]

    def pv(j, slot, alphas):
        c = chunk(qi, j)
        vt = [vt_ref[0, 0, c * sub + n] for n in range(sub)]
        return [_pv_stage(vt, p_sc.at[2 * slot + mi], acc_sc.at[mi], None if alphas is None else alphas[mi])
                for mi in range(2)]

    def restart():
        m_sc[...] = jnp.full_like(m_sc, NEG_INF)

    def finish(totals):
        lamp = lam_ref[...]
        linit = linit_ref[...]
        lam = (jnp.exp(jnp.sum(lamp[0:1] * lamp[1:2], axis=-1, keepdims=True))
               - jnp.exp(jnp.sum(lamp[2:3] * lamp[3:4], axis=-1, keepdims=True)) + linit)
        nv = 2 * DIFF_DIM
        a1, a2 = totals
        o = a1[:nv] * (1.0 / a1[nv:nv + 1]) - a2[:nv] * (lam / a2[nv:nv + 1])
        o = o * lax.rsqrt(jnp.mean(o * o, axis=0, keepdims=True) + NORM_EPS)
        o_ref[0] = (o.T * (g_ref[...] * (1.0 - linit))).astype(o_ref.dtype)
        acc_sc[...] = jnp.zeros_like(acc_sc)

    _tile_pipeline(qi, q1t_ref.shape[3] // tq, nk, *_head_flags(), setup=setup, setup_next=setup_next,
                   qk=qk, sm=sm, pv=pv, restart=restart, finish=finish)


def _diff_attention(layer, q1t, q2t, k1, k2, vt, lamp, linit, subln_g):
    b, hds, _, s = q1t.shape
    nc = vt.shape[2]
    t = DIFF_TILE
    assert t % KV_CHUNK == 0 and s % (2 * t) == 0
    f32, bf = jnp.float32, jnp.bfloat16
    qspec = pl.BlockSpec((1, 1, LANE, s), lambda bb, h, i: (bb, h, 0, 0))
    kspec = pl.BlockSpec((1, 1, s, LANE), lambda bb, h, i: (bb, h, 0, 0))
    const = lambda shape: _layer_block(shape, layer)
    qnext = pl.BlockSpec((1, 1, LANE, t), lambda bb, h, i: (*_next_head(bb, h, hds, b), 0, 0),
                         pipeline_mode=pl.Buffered(1))
    knext = pl.BlockSpec((1, 1, 2 * t, LANE), lambda bb, h, i: (*_next_head(bb, h, hds, b), 0, 0),
                         pipeline_mode=pl.Buffered(1))
    return pl.pallas_call(
        _diff_kernel,
        grid=(b, hds, s // t),
        in_specs=[
            qspec, qspec, kspec, kspec,
            pl.BlockSpec((1, 1, nc, V_ROWS, KV_CHUNK), lambda bb, h, i: (bb, h, 0, 0, 0)),
            qnext, qnext, knext, knext,
            const((4, DIFF_DIM)), const((1, 1)), const((1, 2 * DIFF_DIM)),
        ],
        out_specs=pl.BlockSpec((1, t, 2 * DIFF_DIM), lambda bb, h, i: (bb, i, h)),
        out_shape=jax.ShapeDtypeStruct((b, s, hds * 2 * DIFF_DIM), bf),
        scratch_shapes=[
            pltpu.VMEM((4, LANE, t), bf),
            pltpu.VMEM((t, t), f32),
            pltpu.VMEM((4, t, t), f32),
            pltpu.VMEM((4, t, t), bf),
            pltpu.VMEM((4, 1, t), f32),
            pltpu.VMEM((2, 1, t), f32),
            pltpu.VMEM((2, V_ROWS, t), f32),
        ],
        compiler_params=pltpu.CompilerParams(
            dimension_semantics=("arbitrary", "arbitrary", "arbitrary"), vmem_limit_bytes=VMEM_LIMIT),
        name="diff_attn",
    )(q1t, q2t, k1, k2, vt, q1t, q2t, k1, k2, lamp, linit, subln_g)


def _swap_halves(w):
    half = w.shape[-1] // 2
    return jnp.concatenate([w[..., half:], w[..., :half]], axis=-1)


def _prep_w_in(w_in):
    w = w_in.astype(jnp.bfloat16)
    rope_end = MLA_Q_RANK + MLA_KV_RANK + MLA_ROPE
    k_rope = w[..., rope_end - MLA_ROPE:rope_end]
    return jnp.concatenate([w[..., :rope_end], _swap_halves(k_rope), w[..., rope_end:]], axis=-1)


def _prep_w_q_up(w_q_up):
    l, r, _ = w_q_up.shape
    w = w_q_up.astype(jnp.bfloat16).reshape(l, r, MLA_HEADS, MLA_NOPE + MLA_ROPE)
    w = jnp.concatenate([w, _swap_halves(w[..., MLA_NOPE:])], axis=-1)
    return w.reshape(l, r, MLA_HEADS * MLA_QK)


def _rope_table(s):
    inv_freq = np.float32(ROPE_THETA) ** (-np.arange(0, MLA_ROPE, 2, dtype=np.float32) / np.float32(MLA_ROPE))
    ang = np.arange(s, dtype=np.float32)[:, None] * inv_freq[None, :]
    cos, sin = np.cos(ang), np.sin(ang)
    return jnp.asarray(np.concatenate([cos, cos, -sin, sin], axis=-1), jnp.float32)


def kernel(x, ffn1_w_gate, ffn1_w_up, ffn1_w_down, ln1_g, ln1_b, w_in, q_norm_g, w_q_up, kv_norm_g, w_kv_up, diff_lambda_q1, diff_lambda_k1, diff_lambda_q2, diff_lambda_k2, diff_subln_g, w_out, ln2_g, ln2_b, ffn2_w_gate, ffn2_w_up, ffn2_w_down, ln3_g, ln3_b):
    b, s, d = x.shape
    bf = jnp.bfloat16
    rope = _rope_table(s)
    w_in_p = _prep_w_in(w_in)
    w_q_p = _prep_w_q_up(w_q_up)
    w_kv_p = w_kv_up.astype(bf)
    w_out_p = w_out.astype(bf)
    f1 = (ffn1_w_gate.astype(bf), ffn1_w_up.astype(bf), ffn1_w_down.astype(bf))
    f2 = (ffn2_w_gate.astype(bf), ffn2_w_up.astype(bf), ffn2_w_down.astype(bf))
    lamp = jnp.stack([diff_lambda_q1, diff_lambda_k1, diff_lambda_q2, diff_lambda_k2], axis=1)
    linit = jnp.asarray([[[0.8 - 0.6 * math.exp(-0.3 * l)]] for l in range(DEPTH)], jnp.float32)
    rows = lambda v: v[:, None, :]

    xf = x.reshape(b * s, d)
    for l in range(DEPTH):
        xf = _ffn_ln(l, xf, *f1, rows(ln1_g), rows(ln1_b))
        qt, km, vtm, q1t, q2t, k1, k2, vtd = _inproj(
            l, xf.reshape(b, s, d), w_in_p, rows(q_norm_g), w_q_p, rows(kv_norm_g), w_kv_p, rope)
        o_mla = _mla_attention(qt, km, vtm)
        o_diff = _diff_attention(l, q1t, q2t, k1, k2, vtd, lamp, linit, rows(diff_subln_g))
        xf = _mix_ffn_ln(l, xf, o_mla.reshape(b * s, -1), o_diff.reshape(b * s, -1), w_out_p,
                         rows(ln2_g), rows(ln2_b), *f2, rows(ln3_g), rows(ln3_b))
    return xf.reshape(b, s, d)
```

```python
import math

import jax
import jax.numpy as jnp
import numpy as np
from jax import lax
from jax.experimental import pallas as pl
from jax.experimental.pallas import tpu as pltpu

D_MODEL = 1024
DEPTH = 4
MLA_HEADS = 4
MLA_NOPE = 128
MLA_ROPE = 64
MLA_V = 128
MLA_Q_RANK = 256
MLA_KV_RANK = 128
DIFF_HEADS = 4
DIFF_DIM = 64
D_FF = 2816
ROPE_THETA = 10000.0
NORM_EPS = 1e-5
ALPHA = (2 * DEPTH) ** 0.25

LANE = 128
ROW_TILE = 512
KV_CHUNK = ROW_TILE
DIFF_TILE = 1024
MLA_Q_TILE = 2048
MLA_K_TILE = 1024
POS_SPLIT = 128
VMEM_V7X = 64 * 1024 * 1024
VMEM_LIMIT = VMEM_V7X - 6 * 1024 * 1024

MLA_QK = MLA_NOPE + 2 * MLA_ROPE
MLA_KV = MLA_NOPE + MLA_V
DIFF_W = DIFF_HEADS * 2 * DIFF_DIM
W_IN_SPLITS = (MLA_Q_RANK, MLA_KV_RANK, 2 * MLA_ROPE, DIFF_W, DIFF_W, DIFF_W)
W_IN_COLS = sum(W_IN_SPLITS)
NEG_INF = float("-inf")
LOG2E = math.log2(math.e)
AUG_ROWS = 16
V_ROWS = MLA_V + AUG_ROWS


def _bf16_pieces(x, n):
    out = []
    for _ in range(n):
        piece = float(np.asarray(x, dtype=jnp.bfloat16).astype(np.float32))
        out.append(piece)
        x -= piece
    return tuple(out)


LOG2E_PIECES = _bf16_pieces(LOG2E, 3)
ALIBI_TERMS = 4
ALIBI_SLOTS = len(LOG2E_PIECES) * ALIBI_TERMS
assert ALIBI_SLOTS <= AUG_ROWS


def _dot(a, b):
    return jnp.dot(a, b, preferred_element_type=jnp.float32)


def _layer_norm(y, g, b):
    mu = jnp.mean(y, axis=-1, keepdims=True)
    d = y - mu
    var = jnp.mean(d * d, axis=-1, keepdims=True)
    return d * lax.rsqrt(var + NORM_EPS) * g + b


def _rms_norm(y, g):
    return y * lax.rsqrt(jnp.mean(y * y, axis=-1, keepdims=True) + NORM_EPS) * g


def _ffn_ln_rows(x, wg_ref, wu_ref, wd_ref, g_ref, b_ref):
    xb = x.astype(jnp.bfloat16)
    gate = _dot(xb, wg_ref[...])
    up = _dot(xb, wu_ref[...])
    h = gate / (1.0 + jnp.exp(-gate)) * up
    ffn = _dot(h.astype(jnp.bfloat16), wd_ref[...])
    return _layer_norm(ALPHA * x + 0.5 * ffn, g_ref[...], b_ref[...])


def _row_halves(n):
    return [slice(0, n // 2), slice(n // 2, n)]


def _ffn_ln_kernel(x_ref, wg_ref, wu_ref, wd_ref, g_ref, b_ref, o_ref):
    for rows in _row_halves(x_ref.shape[0]):
        o_ref[rows, :] = _ffn_ln_rows(x_ref[rows, :], wg_ref, wu_ref, wd_ref, g_ref, b_ref)


def _mix_ffn_ln_kernel(x_ref, om_ref, od_ref, wo_ref, g2_ref, b2_ref,
                       wg_ref, wu_ref, wd_ref, g3_ref, b3_ref, o_ref):
    nm = om_ref.shape[1]
    mix = _dot(om_ref[...], wo_ref[:nm, :]) + _dot(od_ref[...], wo_ref[nm:, :])
    x = _layer_norm(ALPHA * x_ref[...] + mix, g2_ref[...], b2_ref[...])
    o_ref[...] = _ffn_ln_rows(x, wg_ref, wu_ref, wd_ref, g3_ref, b3_ref)


def _row_tile(width):
    return pl.BlockSpec((ROW_TILE, width), lambda i: (i, 0))


def _layer_block(shape, layer):
    return pl.BlockSpec((None,) + tuple(shape), lambda *_: (layer,) + (0,) * len(shape),
                        pipeline_mode=pl.Buffered(1))


def _ffn_specs(layer):
    return [_layer_block((D_MODEL, D_FF), layer), _layer_block((D_MODEL, D_FF), layer),
            _layer_block((D_FF, D_MODEL), layer), _layer_block((1, D_MODEL), layer), _layer_block((1, D_MODEL), layer)]


def _ffn_ln(layer, x, wg, wu, wd, g, b):
    t = x.shape[0]
    return pl.pallas_call(
        _ffn_ln_kernel,
        grid=(t // ROW_TILE,),
        in_specs=[_row_tile(D_MODEL)] + _ffn_specs(layer),
        out_specs=_row_tile(D_MODEL),
        out_shape=jax.ShapeDtypeStruct((t, D_MODEL), jnp.float32),
        compiler_params=pltpu.CompilerParams(
            dimension_semantics=("parallel",), vmem_limit_bytes=VMEM_LIMIT),
        name="ffn_ln",
    )(x, wg, wu, wd, g, b)


def _mix_ffn_ln(layer, x, om, od, wo, g2, b2, wg, wu, wd, g3, b3):
    t = x.shape[0]
    nm, nd = om.shape[1], od.shape[1]
    return pl.pallas_call(
        _mix_ffn_ln_kernel,
        grid=(t // ROW_TILE,),
        in_specs=[_row_tile(D_MODEL), _row_tile(nm), _row_tile(nd), _layer_block((nm + nd, D_MODEL), layer),
                  _layer_block((1, D_MODEL), layer), _layer_block((1, D_MODEL), layer)] + _ffn_specs(layer),
        out_specs=_row_tile(D_MODEL),
        out_shape=jax.ShapeDtypeStruct((t, D_MODEL), jnp.float32),
        compiler_params=pltpu.CompilerParams(
            dimension_semantics=("parallel",), vmem_limit_bytes=VMEM_LIMIT),
        name="mix_ffn_ln",
    )(x, om, od, wo, g2, b2, wg, wu, wd, g3, b3)


def _alibi_slope(h):
    return 2.0 ** (-8.0 * (h + 1) / DIFF_HEADS)


def _inproj_kernel(x_ref, win_ref, qg_ref, wq_ref, kvg_ref, wkv_ref, rope_ref,
                   qt_ref, km_ref, vtm_ref, q1t_ref, q2t_ref, k1_ref, k2_ref, vtd_ref):
    i = pl.program_id(1)
    tm = x_ref.shape[1]
    xb = x_ref[0].astype(jnp.bfloat16)
    h = _dot(xb, win_ref[...])
    edges = np.cumsum((0,) + W_IN_SPLITS)
    c_q, c_kv, kr, dq, dk, dv = (h[:, int(a):int(b)] for a, b in zip(edges[:-1], edges[1:]))
    rope = rope_ref[...]

    q = _dot(_rms_norm(c_q, qg_ref[...]).astype(jnp.bfloat16), wq_ref[...])
    kv = _dot(_rms_norm(c_kv, kvg_ref[...]).astype(jnp.bfloat16), wkv_ref[...])
    krt = kr * rope
    k_rope = krt + pltpu.roll(krt, MLA_ROPE, axis=1)
    mla_scale = (MLA_NOPE + MLA_ROPE) ** -0.5 * LOG2E
    ones_rows = jnp.where(lax.broadcasted_iota(jnp.int32, (AUG_ROWS, tm), 0) == 0, 1.0, 0.0)
    for hh in range(MLA_HEADS):
        qh = q[:, hh * MLA_QK:(hh + 1) * MLA_QK]
        qh = jnp.concatenate([qh[:, :MLA_NOPE], qh[:, MLA_NOPE:] * rope], axis=1) * mla_scale
        qt_ref[0, hh] = qh.T.astype(jnp.bfloat16)
        kvh = kv[:, hh * MLA_KV:(hh + 1) * MLA_KV]
        km_ref[0, hh] = jnp.concatenate([kvh[:, :MLA_NOPE], k_rope], axis=1).astype(jnp.bfloat16)
        vtm_ref[0, hh, 0] = jnp.concatenate([kvh[:, MLA_NOPE:].T, ones_rows], axis=0).astype(jnp.bfloat16)

    pos_c = i * tm + lax.broadcasted_iota(jnp.int32, (AUG_ROWS, tm), 1)
    row = lax.broadcasted_iota(jnp.int32, (AUG_ROWS, tm), 0)
    hi_c = (pos_c & -POS_SPLIT).astype(jnp.float32)
    lo_c = (pos_c & (POS_SPLIT - 1)).astype(jnp.float32)
    pos_r = i * tm + lax.broadcasted_iota(jnp.int32, (tm, LANE), 0)
    lane = lax.broadcasted_iota(jnp.int32, (tm, LANE), 1)
    hi_r = (pos_r & -POS_SPLIT).astype(jnp.float32)
    lo_r = (pos_r & (POS_SPLIT - 1)).astype(jnp.float32)
    diff_scale = DIFF_DIM ** -0.5 * LOG2E
    zpad = jnp.zeros((DIFF_DIM - AUG_ROWS, tm), jnp.float32)
    def piece(slot):
        return jnp.where(slot < ALIBI_TERMS, LOG2E_PIECES[0],
                         jnp.where(slot < 2 * ALIBI_TERMS, LOG2E_PIECES[1], LOG2E_PIECES[2]))

    slot_r = lane - DIFF_DIM
    piece_c, term_c = piece(row), row & (ALIBI_TERMS - 1)
    piece_r, term_r = piece(slot_r), slot_r & (ALIBI_TERMS - 1)
    for hh in range(DIFF_HEADS):
        slope = _alibi_slope(hh)
        qaug = jnp.where(row >= ALIBI_SLOTS, 0.0,
                         jnp.where(term_c == 0, -slope * hi_c,
                                   jnp.where(term_c == 1, -slope * lo_c, piece_c)))
        kaug = jnp.where(slot_r >= ALIBI_SLOTS, 0.0,
                         jnp.where(term_r < 2, piece_r,
                                   jnp.where(term_r == 2, slope * hi_r, slope * lo_r)))
        per_head = slice(hh * 2 * DIFF_DIM, (hh + 1) * 2 * DIFF_DIM)
        tq = (dq[:, per_head] * diff_scale).T
        q1t_ref[0, hh] = jnp.concatenate([tq[:DIFF_DIM], qaug, zpad], axis=0).astype(jnp.bfloat16)
        q2t_ref[0, hh] = jnp.concatenate([tq[DIFF_DIM:], qaug, zpad], axis=0).astype(jnp.bfloat16)
        tk = dk[:, per_head]
        k1_ref[0, hh] = jnp.where(lane < DIFF_DIM, tk, kaug).astype(jnp.bfloat16)
        k2_ref[0, hh] = jnp.where(lane < DIFF_DIM, pltpu.roll(tk, DIFF_DIM, axis=1), kaug).astype(jnp.bfloat16)
        vtd_ref[0, hh, 0] = jnp.concatenate(
            [dv[:, per_head].T, ones_rows], axis=0).astype(jnp.bfloat16)


def _inproj(layer, x, win, qg, wq, kvg, wkv, rope):
    b, s, _ = x.shape
    nc = s // KV_CHUNK
    tm = ROW_TILE
    bf = jnp.bfloat16
    const = lambda shape: _layer_block(shape, layer)
    out_shape = (
        jax.ShapeDtypeStruct((b, MLA_HEADS, MLA_QK, s), bf),
        jax.ShapeDtypeStruct((b, MLA_HEADS, s, MLA_QK), bf),
        jax.ShapeDtypeStruct((b, MLA_HEADS, nc, V_ROWS, KV_CHUNK), bf),
        jax.ShapeDtypeStruct((b, DIFF_HEADS, LANE, s), bf),
        jax.ShapeDtypeStruct((b, DIFF_HEADS, LANE, s), bf),
        jax.ShapeDtypeStruct((b, DIFF_HEADS, s, LANE), bf),
        jax.ShapeDtypeStruct((b, DIFF_HEADS, s, LANE), bf),
        jax.ShapeDtypeStruct((b, DIFF_HEADS, nc, V_ROWS, KV_CHUNK), bf),
    )
    out_specs = (
        pl.BlockSpec((1, MLA_HEADS, MLA_QK, tm), lambda bb, i: (bb, 0, 0, i)),
        pl.BlockSpec((1, MLA_HEADS, tm, MLA_QK), lambda bb, i: (bb, 0, i, 0)),
        pl.BlockSpec((1, MLA_HEADS, 1, V_ROWS, tm), lambda bb, i: (bb, 0, i, 0, 0)),
        pl.BlockSpec((1, DIFF_HEADS, LANE, tm), lambda bb, i: (bb, 0, 0, i)),
        pl.BlockSpec((1, DIFF_HEADS, LANE, tm), lambda bb, i: (bb, 0, 0, i)),
        pl.BlockSpec((1, DIFF_HEADS, tm, LANE), lambda bb, i: (bb, 0, i, 0)),
        pl.BlockSpec((1, DIFF_HEADS, tm, LANE), lambda bb, i: (bb, 0, i, 0)),
        pl.BlockSpec((1, DIFF_HEADS, 1, V_ROWS, tm), lambda bb, i: (bb, 0, i, 0, 0)),
    )
    return pl.pallas_call(
        _inproj_kernel,
        grid=(b, s // tm),
        in_specs=[
            pl.BlockSpec((1, tm, D_MODEL), lambda bb, i: (bb, i, 0)),
            const((D_MODEL, W_IN_COLS)),
            const((1, MLA_Q_RANK)),
            const((MLA_Q_RANK, MLA_HEADS * MLA_QK)),
            const((1, MLA_KV_RANK)),
            const((MLA_KV_RANK, MLA_HEADS * MLA_KV)),
            pl.BlockSpec((tm, LANE), lambda bb, i: (i, 0)),
        ],
        out_specs=out_specs,
        out_shape=out_shape,
        compiler_params=pltpu.CompilerParams(
            dimension_semantics=("parallel", "parallel"), vmem_limit_bytes=VMEM_LIMIT),
        name="inproj",
    )(x, win, qg, wq, kvg, wkv, rope)


def _qk_stage(k_tile, q_tile, s_ref, mb_ref, corr=None):
    s = _dot(k_tile, q_tile)
    if corr is not None:
        s = s - corr
    s_ref[...] = s
    mb_ref[...] = jnp.max(s, axis=0, keepdims=True)


def _sm_stage(s_ref, mb_ref, p_ref, m_ref):
    m_old = m_ref[...]
    m_new = jnp.maximum(m_old, mb_ref[...])
    p_ref[...] = jnp.exp2(s_ref[...] - m_new).astype(p_ref.dtype)
    m_ref[...] = m_new
    return jnp.exp2(m_old - m_new)


def _pv_stage(vt_tiles, p_ref, acc_ref, alpha):
    acc = acc_ref[...]
    for n, vt_tile in enumerate(vt_tiles):
        acc = acc + _dot(vt_tile, p_ref[n * KV_CHUNK:(n + 1) * KV_CHUNK, :])
    if alpha is None:
        return acc
    acc_ref[...] = alpha * acc


def _key_rows(c, tk):
    return pl.ds(pl.multiple_of(c * tk, tk), tk)


def _query_cols(tile, tq):
    return pl.ds(pl.multiple_of(tile * tq, tq), tq)


def _tile_pipeline(qi, nq, nk, *, setup, qk, sm, pv, restart, finish, lookahead=False):
    last_tile = qi == nq - 1
    first_head, last_head = _head_flags() if lookahead else (True, True)

    @pl.when(jnp.logical_and(first_head, qi == 0))
    def _():
        setup()
        restart()
        qk(0, 0, 0, False)
        qk(0, 1, 1, False)
        sm(0)

    def step(j, slot):
        qk(qi, j + 1, 1 - slot, False)
        pv(j - 1, 1 - slot, sm(slot))

    def pair(t, carry):
        step(2 * t + 1, 1)
        step(2 * t + 2, 0)
        return carry

    lax.fori_loop(0, nk // 2 - 1, pair, 0)

    def hand_over(tile, nxt):
        qk(tile, 0, 0, nxt)
        pv(nk - 2, 0, sm(1))
        qk(tile, 1, 1, nxt)
        restart()
        sm(0)
        finish(pv(nk - 1, 1, None))

    @pl.when(jnp.logical_not(last_tile))
    def _():
        hand_over(qi + 1, False)

    if lookahead:
        @pl.when(jnp.logical_and(last_tile, jnp.logical_not(last_head)))
        def _():
            hand_over(0, True)

    @pl.when(jnp.logical_and(last_tile, last_head))
    def _():
        pv(nk - 2, 0, sm(1))
        finish(pv(nk - 1, 1, None))


def _head_flags():
    b, h = pl.program_id(0), pl.program_id(1)
    first = jnp.logical_and(b == 0, h == 0)
    last = jnp.logical_and(b == pl.num_programs(0) - 1, h == pl.num_programs(1) - 1)
    return first, last


def _next_head(bb, h, n_heads, n_batch):
    flat = jnp.minimum(bb * n_heads + h + 1, n_batch * n_heads - 1)
    return flat // n_heads, flat % n_heads


def _mla_kernel(qt_ref, k_ref, vt_ref, qt_nx_ref, k_nx_ref, o_ref, s0, s1, p0, p1, mb0, mb1, m_sc, acc_sc):
    qi = pl.program_id(2)
    tk, tq = s0.shape
    sub = tk // KV_CHUNK
    nk = vt_ref.shape[2] // sub
    s_bufs, p_bufs, mb_bufs = (s0, s1), (p0, p1), (mb0, mb1)
    halves = [pl.ds(i * (tq // 2), tq // 2) for i in range(2)]

    def setup():
        acc_sc[...] = jnp.zeros_like(acc_sc)

    def qk(tile, j, slot, nxt):
        q_src, k_src = (qt_nx_ref, k_nx_ref) if nxt else (qt_ref, k_ref)
        for i, hv in enumerate(halves):
            cols = pl.ds(pl.multiple_of(tile * tq + i * (tq // 2), tq // 2), tq // 2)
            _qk_stage(k_src[0, 0, _key_rows(j, tk), :], q_src[0, 0, :, cols],
                      s_bufs[slot].at[:, hv], mb_bufs[slot].at[:, hv])

    def sm(slot):
        return [_sm_stage(s_bufs[slot].at[:, hv], mb_bufs[slot].at[:, hv], p_bufs[slot].at[:, hv],
                          m_sc.at[:, hv]) for hv in halves]

    def pv(j, slot, alphas):
        vt = [vt_ref[0, 0, j * sub + n] for n in range(sub)]
        return [_pv_stage(vt, p_bufs[slot].at[:, hv], acc_sc.at[:, hv], None if alphas is None else alphas[i])
                for i, hv in enumerate(halves)]

    def restart():
        m_sc[...] = jnp.full_like(m_sc, NEG_INF)

    def finish(totals):
        acc = jnp.concatenate(totals, axis=1)
        o = acc[:MLA_V] / acc[MLA_V:MLA_V + 1]
        o_ref[0] = o.T.astype(o_ref.dtype)
        acc_sc[...] = jnp.zeros_like(acc_sc)

    _tile_pipeline(qi, qt_ref.shape[3] // tq, nk, setup=setup, qk=qk, sm=sm, pv=pv, restart=restart, finish=finish,
                   lookahead=True)


def _mla_attention(qt, k, vt):
    b, hds, _, s = qt.shape
    nc = vt.shape[2]
    tq, tk = MLA_Q_TILE, MLA_K_TILE
    assert tk % KV_CHUNK == 0 and s % (2 * tk) == 0 and s % tq == 0
    f32, bf = jnp.float32, jnp.bfloat16
    return pl.pallas_call(
        _mla_kernel,
        grid=(b, hds, s // tq),
        in_specs=[
            pl.BlockSpec((1, 1, MLA_QK, s), lambda bb, h, i: (bb, h, 0, 0)),
            pl.BlockSpec((1, 1, s, MLA_QK), lambda bb, h, i: (bb, h, 0, 0)),
            pl.BlockSpec((1, 1, nc, V_ROWS, KV_CHUNK), lambda bb, h, i: (bb, h, 0, 0, 0)),
            pl.BlockSpec((1, 1, MLA_QK, tq), lambda bb, h, i: (*_next_head(bb, h, hds, b), 0, 0)),
            pl.BlockSpec((1, 1, 2 * tk, MLA_QK), lambda bb, h, i: (*_next_head(bb, h, hds, b), 0, 0)),
        ],
        out_specs=pl.BlockSpec((1, tq, MLA_V), lambda bb, h, i: (bb, i, h)),
        out_shape=jax.ShapeDtypeStruct((b, s, hds * MLA_V), bf),
        scratch_shapes=[
            pltpu.VMEM((tk, tq), f32), pltpu.VMEM((tk, tq), f32),
            pltpu.VMEM((tk, tq), bf), pltpu.VMEM((tk, tq), bf),
            pltpu.VMEM((1, tq), f32), pltpu.VMEM((1, tq), f32),
            pltpu.VMEM((1, tq), f32),
            pltpu.VMEM((V_ROWS, tq), f32),
        ],
        compiler_params=pltpu.CompilerParams(
            dimension_semantics=("arbitrary", "arbitrary", "arbitrary"), vmem_limit_bytes=VMEM_LIMIT),
        name="mla_attn",
    )(qt, k, vt, qt, k)


def _diff_kernel(q1t_ref, q2t_ref, k1_ref, k2_ref, vt_ref, lam_ref, linit_ref, g_ref, o_ref,
                 qv_sc, corr_sc, s_sc, p_sc, mb_sc, m_sc, acc_sc):
    h = pl.program_id(1)
    qi = pl.program_id(2)
    tq = o_ref.shape[1]
    sub = tq // KV_CHUNK
    nk = vt_ref.shape[2] // sub
    k_refs = (k1_ref, k2_ref)
    q_refs = (q1t_ref, q2t_ref)

    def setup():
        jj = lax.broadcasted_iota(jnp.int32, (tq, tq), 0)
        ii = lax.broadcasted_iota(jnp.int32, (tq, tq), 1)
        expo = jnp.full((1, tq), 127 + 1, jnp.int32) - 2 * (h + 1)
        slope2 = lax.bitcast_convert_type(expo << 23, jnp.float32)
        corr_sc[...] = (slope2 * LOG2E) * jnp.maximum(jj - ii, 0).astype(jnp.float32)
        acc_sc[...] = jnp.zeros_like(acc_sc)

    def chunk(tile, j):
        return lax.rem(tile + j, nk)

    def load_queries(tile):
        row = lax.broadcasted_iota(jnp.int32, (LANE, tq), 0)
        for mi in range(2):
            qq = q_refs[mi][0, 0, :, _query_cols(tile, tq)]
            qv_sc[mi] = qq
            qv_sc[2 + mi] = jnp.where(row < DIFF_DIM, qq, -qq)

    def qk(tile, j, slot, nxt):
        assert not nxt
        diagonal = isinstance(j, int) and j == 0
        if diagonal:
            load_queries(tile)
        c = chunk(tile, j)
        variant = 0 if diagonal else jnp.where(c > tile, 2, 0)
        for mi in range(2):
            _qk_stage(k_refs[mi][0, 0, _key_rows(c, tq), :], qv_sc[variant + mi],
                      s_sc.at[2 * slot + mi], mb_sc.at[2 * slot + mi],
                      corr_sc[...] if diagonal else None)

    def sm(slot):
        return [_sm_stage(s_sc.at[2 * slot + mi], mb_sc.at[2 * slot + mi], p_sc.at[2 * slot + mi],
                          m_sc.at[mi]) for mi in range(2)]

    def pv(j, slot, alphas):
        c = chunk(qi, j)
        vt = [vt_ref[0, 0, c * sub + n] for n in range(sub)]
        return [_pv_stage(vt, p_sc.at[2 * slot + mi], acc_sc.at[mi], None if alphas is None else alphas[mi])
                for mi in range(2)]

    def restart():
        m_sc[...] = jnp.full_like(m_sc, NEG_INF)

    def finish(totals):
        lamp = lam_ref[...]
        linit = linit_ref[...]
        lam = (jnp.exp(jnp.sum(lamp[0:1] * lamp[1:2], axis=-1, keepdims=True))
               - jnp.exp(jnp.sum(lamp[2:3] * lamp[3:4], axis=-1, keepdims=True)) + linit)
        nv = 2 * DIFF_DIM
        a1, a2 = totals
        o = a1[:nv] * (1.0 / a1[nv:nv + 1]) - a2[:nv] * (lam / a2[nv:nv + 1])
        o = o * lax.rsqrt(jnp.mean(o * o, axis=0, keepdims=True) + NORM_EPS)
        o_ref[0] = (o.T * (g_ref[...] * (1.0 - linit))).astype(o_ref.dtype)
        acc_sc[...] = jnp.zeros_like(acc_sc)

    _tile_pipeline(qi, q1t_ref.shape[3] // tq, nk, setup=setup, qk=qk, sm=sm, pv=pv, restart=restart, finish=finish)


def _diff_attention(layer, q1t, q2t, k1, k2, vt, lamp, linit, subln_g):
    b, hds, _, s = q1t.shape
    nc = vt.shape[2]
    t = DIFF_TILE
    assert t % KV_CHUNK == 0 and s % (2 * t) == 0
    f32, bf = jnp.float32, jnp.bfloat16
    qspec = pl.BlockSpec((1, 1, LANE, s), lambda bb, h, i: (bb, h, 0, 0))
    kspec = pl.BlockSpec((1, 1, s, LANE), lambda bb, h, i: (bb, h, 0, 0))
    const = lambda shape: _layer_block(shape, layer)
    return pl.pallas_call(
        _diff_kernel,
        grid=(b, hds, s // t),
        in_specs=[
            qspec, qspec, kspec, kspec,
            pl.BlockSpec((1, 1, nc, V_ROWS, KV_CHUNK), lambda bb, h, i: (bb, h, 0, 0, 0)),
            const((4, DIFF_DIM)), const((1, 1)), const((1, 2 * DIFF_DIM)),
        ],
        out_specs=pl.BlockSpec((1, t, 2 * DIFF_DIM), lambda bb, h, i: (bb, i, h)),
        out_shape=jax.ShapeDtypeStruct((b, s, hds * 2 * DIFF_DIM), bf),
        scratch_shapes=[
            pltpu.VMEM((4, LANE, t), bf),
            pltpu.VMEM((t, t), f32),
            pltpu.VMEM((4, t, t), f32),
            pltpu.VMEM((4, t, t), bf),
            pltpu.VMEM((4, 1, t), f32),
            pltpu.VMEM((2, 1, t), f32),
            pltpu.VMEM((2, V_ROWS, t), f32),
        ],
        compiler_params=pltpu.CompilerParams(
            dimension_semantics=("parallel", "parallel", "arbitrary"), vmem_limit_bytes=VMEM_LIMIT),
        name="diff_attn",
    )(q1t, q2t, k1, k2, vt, lamp, linit, subln_g)


def _swap_halves(w):
    half = w.shape[-1] // 2
    return jnp.concatenate([w[..., half:], w[..., :half]], axis=-1)


def _prep_w_in(w_in):
    w = w_in.astype(jnp.bfloat16)
    rope_end = MLA_Q_RANK + MLA_KV_RANK + MLA_ROPE
    k_rope = w[..., rope_end - MLA_ROPE:rope_end]
    return jnp.concatenate([w[..., :rope_end], _swap_halves(k_rope), w[..., rope_end:]], axis=-1)


def _prep_w_q_up(w_q_up):
    l, r, _ = w_q_up.shape
    w = w_q_up.astype(jnp.bfloat16).reshape(l, r, MLA_HEADS, MLA_NOPE + MLA_ROPE)
    w = jnp.concatenate([w, _swap_halves(w[..., MLA_NOPE:])], axis=-1)
    return w.reshape(l, r, MLA_HEADS * MLA_QK)


def _rope_table(s):
    inv_freq = np.float32(ROPE_THETA) ** (-np.arange(0, MLA_ROPE, 2, dtype=np.float32) / np.float32(MLA_ROPE))
    ang = np.arange(s, dtype=np.float32)[:, None] * inv_freq[None, :]
    cos, sin = np.cos(ang), np.sin(ang)
    return jnp.asarray(np.concatenate([cos, cos, -sin, sin], axis=-1), jnp.float32)


def kernel(x, ffn1_w_gate, ffn1_w_up, ffn1_w_down, ln1_g, ln1_b, w_in, q_norm_g, w_q_up, kv_norm_g, w_kv_up, diff_lambda_q1, diff_lambda_k1, diff_lambda_q2, diff_lambda_k2, diff_subln_g, w_out, ln2_g, ln2_b, ffn2_w_gate, ffn2_w_up, ffn2_w_down, ln3_g, ln3_b):
    b, s, d = x.shape
    bf = jnp.bfloat16
    rope = _rope_table(s)
    w_in_p = _prep_w_in(w_in)
    w_q_p = _prep_w_q_up(w_q_up)
    w_kv_p = w_kv_up.astype(bf)
    w_out_p = w_out.astype(bf)
    f1 = (ffn1_w_gate.astype(bf), ffn1_w_up.astype(bf), ffn1_w_down.astype(bf))
    f2 = (ffn2_w_gate.astype(bf), ffn2_w_up.astype(bf), ffn2_w_down.astype(bf))
    lamp = jnp.stack([diff_lambda_q1, diff_lambda_k1, diff_lambda_q2, diff_lambda_k2], axis=1)
    linit = jnp.asarray([[[0.8 - 0.6 * math.exp(-0.3 * l)]] for l in range(DEPTH)], jnp.float32)
    rows = lambda v: v[:, None, :]

    xf = x.reshape(b * s, d)
    for l in range(DEPTH):
        xf = _ffn_ln(l, xf, *f1, rows(ln1_g), rows(ln1_b))
        qt, km, vtm, q1t, q2t, k1, k2, vtd = _inproj(
            l, xf.reshape(b, s, d), w_in_p, rows(q_norm_g), w_q_p, rows(kv_norm_g), w_kv_p, rope)
        o_mla = _mla_attention(qt, km, vtm)
        o_diff = _diff_attention(l, q1t, q2t, k1, k2, vtd, lamp, linit, rows(diff_subln_g))
        xf = _mix_ffn_ln(l, xf, o_mla.reshape(b * s, -1), o_diff.reshape(b * s, -1), w_out_p,
                         rows(ln2_g), rows(ln2_b), *f2, rows(ln3_g), rows(ln3_b))
    return xf.reshape(b, s, d)
```

```python
import math

import jax
import jax.numpy as jnp
import numpy as np
from jax import lax
from jax.experimental import pallas as pl
from jax.experimental.pallas import tpu as pltpu

D_MODEL = 1024
DEPTH = 4
MLA_HEADS = 4
MLA_NOPE = 128
MLA_ROPE = 64
MLA_V = 128
MLA_Q_RANK = 256
MLA_KV_RANK = 128
DIFF_HEADS = 4
DIFF_DIM = 64
D_FF = 2816
ROPE_THETA = 10000.0
NORM_EPS = 1e-5
ALPHA = (2 * DEPTH) ** 0.25

LANE = 128
ROW_TILE = 512
KV_CHUNK = ROW_TILE
DIFF_TILE = 1024
MLA_Q_TILE = 2048
MLA_K_TILE = 1024
POS_SPLIT = 128
VMEM_V7X = 64 * 1024 * 1024
VMEM_LIMIT = VMEM_V7X - 6 * 1024 * 1024

MLA_QK = MLA_NOPE + 2 * MLA_ROPE
MLA_KV = MLA_NOPE + MLA_V
DIFF_W = DIFF_HEADS * 2 * DIFF_DIM
W_IN_SPLITS = (MLA_Q_RANK, MLA_KV_RANK, 2 * MLA_ROPE, DIFF_W, DIFF_W, DIFF_W)
W_IN_COLS = sum(W_IN_SPLITS)
NEG_INF = float("-inf")
LOG2E = math.log2(math.e)
AUG_ROWS = 16
V_ROWS = MLA_V + AUG_ROWS


def _bf16_pieces(x, n):
    out = []
    for _ in range(n):
        piece = float(np.asarray(x, dtype=jnp.bfloat16).astype(np.float32))
        out.append(piece)
        x -= piece
    return tuple(out)


LOG2E_PIECES = _bf16_pieces(LOG2E, 3)
ALIBI_TERMS = 4
ALIBI_SLOTS = len(LOG2E_PIECES) * ALIBI_TERMS
assert ALIBI_SLOTS <= AUG_ROWS


def _dot(a, b):
    return jnp.dot(a, b, preferred_element_type=jnp.float32)


def _layer_norm(y, g, b):
    mu = jnp.mean(y, axis=-1, keepdims=True)
    d = y - mu
    var = jnp.mean(d * d, axis=-1, keepdims=True)
    return d * lax.rsqrt(var + NORM_EPS) * g + b


def _rms_norm(y, g):
    return y * lax.rsqrt(jnp.mean(y * y, axis=-1, keepdims=True) + NORM_EPS) * g


def _ffn_ln_rows(x, wg_ref, wu_ref, wd_ref, g_ref, b_ref):
    xb = x.astype(jnp.bfloat16)
    gate = _dot(xb, wg_ref[...])
    up = _dot(xb, wu_ref[...])
    h = gate / (1.0 + jnp.exp(-gate)) * up
    ffn = _dot(h.astype(jnp.bfloat16), wd_ref[...])
    return _layer_norm(ALPHA * x + 0.5 * ffn, g_ref[...], b_ref[...])


def _row_halves(n):
    return [slice(0, n // 2), slice(n // 2, n)]


def _ffn_ln_kernel(x_ref, wg_ref, wu_ref, wd_ref, g_ref, b_ref, o_ref):
    for rows in _row_halves(x_ref.shape[0]):
        o_ref[rows, :] = _ffn_ln_rows(x_ref[rows, :], wg_ref, wu_ref, wd_ref, g_ref, b_ref)


def _mix_ffn_ln_kernel(x_ref, om_ref, od_ref, wo_ref, g2_ref, b2_ref,
                       wg_ref, wu_ref, wd_ref, g3_ref, b3_ref, o_ref):
    nm = om_ref.shape[1]
    mix = _dot(om_ref[...], wo_ref[:nm, :]) + _dot(od_ref[...], wo_ref[nm:, :])
    x = _layer_norm(ALPHA * x_ref[...] + mix, g2_ref[...], b2_ref[...])
    for rows in _row_halves(x.shape[0]):
        o_ref[rows, :] = _ffn_ln_rows(x[rows, :], wg_ref, wu_ref, wd_ref, g3_ref, b3_ref)


def _row_tile(width):
    return pl.BlockSpec((ROW_TILE, width), lambda i: (i, 0))


def _layer_block(shape, layer):
    return pl.BlockSpec((None,) + tuple(shape), lambda *_: (layer,) + (0,) * len(shape),
                        pipeline_mode=pl.Buffered(1))


def _ffn_specs(layer):
    return [_layer_block((D_MODEL, D_FF), layer), _layer_block((D_MODEL, D_FF), layer),
            _layer_block((D_FF, D_MODEL), layer), _layer_block((1, D_MODEL), layer), _layer_block((1, D_MODEL), layer)]


def _ffn_ln(layer, x, wg, wu, wd, g, b):
    t = x.shape[0]
    return pl.pallas_call(
        _ffn_ln_kernel,
        grid=(t // ROW_TILE,),
        in_specs=[_row_tile(D_MODEL)] + _ffn_specs(layer),
        out_specs=_row_tile(D_MODEL),
        out_shape=jax.ShapeDtypeStruct((t, D_MODEL), jnp.float32),
        compiler_params=pltpu.CompilerParams(
            dimension_semantics=("parallel",), vmem_limit_bytes=VMEM_LIMIT),
        name="ffn_ln",
    )(x, wg, wu, wd, g, b)


def _mix_ffn_ln(layer, x, om, od, wo, g2, b2, wg, wu, wd, g3, b3):
    t = x.shape[0]
    nm, nd = om.shape[1], od.shape[1]
    return pl.pallas_call(
        _mix_ffn_ln_kernel,
        grid=(t // ROW_TILE,),
        in_specs=[_row_tile(D_MODEL), _row_tile(nm), _row_tile(nd), _layer_block((nm + nd, D_MODEL), layer),
                  _layer_block((1, D_MODEL), layer), _layer_block((1, D_MODEL), layer)] + _ffn_specs(layer),
        out_specs=_row_tile(D_MODEL),
        out_shape=jax.ShapeDtypeStruct((t, D_MODEL), jnp.float32),
        compiler_params=pltpu.CompilerParams(
            dimension_semantics=("parallel",), vmem_limit_bytes=VMEM_LIMIT),
        name="mix_ffn_ln",
    )(x, om, od, wo, g2, b2, wg, wu, wd, g3, b3)


def _alibi_slope(h):
    return 2.0 ** (-8.0 * (h + 1) / DIFF_HEADS)


def _inproj_kernel(x_ref, win_ref, qg_ref, wq_ref, kvg_ref, wkv_ref, rope_ref,
                   qt_ref, km_ref, vtm_ref, q1t_ref, q2t_ref, k1_ref, k2_ref, vtd_ref):
    i = pl.program_id(1)
    tm = x_ref.shape[1]
    xb = x_ref[0].astype(jnp.bfloat16)
    h = _dot(xb, win_ref[...])
    edges = np.cumsum((0,) + W_IN_SPLITS)
    c_q, c_kv, kr, dq, dk, dv = (h[:, int(a):int(b)] for a, b in zip(edges[:-1], edges[1:]))
    rope = rope_ref[...]

    q = _dot(_rms_norm(c_q, qg_ref[...]).astype(jnp.bfloat16), wq_ref[...])
    kv = _dot(_rms_norm(c_kv, kvg_ref[...]).astype(jnp.bfloat16), wkv_ref[...])
    krt = kr * rope
    k_rope = krt + pltpu.roll(krt, MLA_ROPE, axis=1)
    mla_scale = (MLA_NOPE + MLA_ROPE) ** -0.5 * LOG2E
    ones_rows = jnp.where(lax.broadcasted_iota(jnp.int32, (AUG_ROWS, tm), 0) == 0, 1.0, 0.0)
    for hh in range(MLA_HEADS):
        qh = q[:, hh * MLA_QK:(hh + 1) * MLA_QK]
        qh = jnp.concatenate([qh[:, :MLA_NOPE], qh[:, MLA_NOPE:] * rope], axis=1) * mla_scale
        qt_ref[0, hh] = qh.T.astype(jnp.bfloat16)
        kvh = kv[:, hh * MLA_KV:(hh + 1) * MLA_KV]
        km_ref[0, hh] = jnp.concatenate([kvh[:, :MLA_NOPE], k_rope], axis=1).astype(jnp.bfloat16)
        vtm_ref[0, hh, 0] = jnp.concatenate([kvh[:, MLA_NOPE:].T, ones_rows], axis=0).astype(jnp.bfloat16)

    pos_c = i * tm + lax.broadcasted_iota(jnp.int32, (AUG_ROWS, tm), 1)
    row = lax.broadcasted_iota(jnp.int32, (AUG_ROWS, tm), 0)
    hi_c = (pos_c & -POS_SPLIT).astype(jnp.float32)
    lo_c = (pos_c & (POS_SPLIT - 1)).astype(jnp.float32)
    pos_r = i * tm + lax.broadcasted_iota(jnp.int32, (tm, LANE), 0)
    lane = lax.broadcasted_iota(jnp.int32, (tm, LANE), 1)
    hi_r = (pos_r & -POS_SPLIT).astype(jnp.float32)
    lo_r = (pos_r & (POS_SPLIT - 1)).astype(jnp.float32)
    diff_scale = DIFF_DIM ** -0.5 * LOG2E
    zpad = jnp.zeros((DIFF_DIM - AUG_ROWS, tm), jnp.float32)
    def piece(slot):
        return jnp.where(slot < ALIBI_TERMS, LOG2E_PIECES[0],
                         jnp.where(slot < 2 * ALIBI_TERMS, LOG2E_PIECES[1], LOG2E_PIECES[2]))

    slot_r = lane - DIFF_DIM
    piece_c, term_c = piece(row), row & (ALIBI_TERMS - 1)
    piece_r, term_r = piece(slot_r), slot_r & (ALIBI_TERMS - 1)
    for hh in range(DIFF_HEADS):
        slope = _alibi_slope(hh)
        qaug = jnp.where(row >= ALIBI_SLOTS, 0.0,
                         jnp.where(term_c == 0, -slope * hi_c,
                                   jnp.where(term_c == 1, -slope * lo_c, piece_c)))
        kaug = jnp.where(slot_r >= ALIBI_SLOTS, 0.0,
                         jnp.where(term_r < 2, piece_r,
                                   jnp.where(term_r == 2, slope * hi_r, slope * lo_r)))
        per_head = slice(hh * 2 * DIFF_DIM, (hh + 1) * 2 * DIFF_DIM)
        tq = (dq[:, per_head] * diff_scale).T
        q1t_ref[0, hh] = jnp.concatenate([tq[:DIFF_DIM], qaug, zpad], axis=0).astype(jnp.bfloat16)
        q2t_ref[0, hh] = jnp.concatenate([tq[DIFF_DIM:], qaug, zpad], axis=0).astype(jnp.bfloat16)
        tk = dk[:, per_head]
        k1_ref[0, hh] = jnp.where(lane < DIFF_DIM, tk, kaug).astype(jnp.bfloat16)
        k2_ref[0, hh] = jnp.where(lane < DIFF_DIM, pltpu.roll(tk, DIFF_DIM, axis=1), kaug).astype(jnp.bfloat16)
        vtd_ref[0, hh, 0] = jnp.concatenate(
            [dv[:, per_head].T, ones_rows], axis=0).astype(jnp.bfloat16)


def _inproj(layer, x, win, qg, wq, kvg, wkv, rope):
    b, s, _ = x.shape
    nc = s // KV_CHUNK
    tm = ROW_TILE
    bf = jnp.bfloat16
    const = lambda shape: _layer_block(shape, layer)
    out_shape = (
        jax.ShapeDtypeStruct((b, MLA_HEADS, MLA_QK, s), bf),
        jax.ShapeDtypeStruct((b, MLA_HEADS, s, MLA_QK), bf),
        jax.ShapeDtypeStruct((b, MLA_HEADS, nc, V_ROWS, KV_CHUNK), bf),
        jax.ShapeDtypeStruct((b, DIFF_HEADS, LANE, s), bf),
        jax.ShapeDtypeStruct((b, DIFF_HEADS, LANE, s), bf),
        jax.ShapeDtypeStruct((b, DIFF_HEADS, s, LANE), bf),
        jax.ShapeDtypeStruct((b, DIFF_HEADS, s, LANE), bf),
        jax.ShapeDtypeStruct((b, DIFF_HEADS, nc, V_ROWS, KV_CHUNK), bf),
    )
    out_specs = (
        pl.BlockSpec((1, MLA_HEADS, MLA_QK, tm), lambda bb, i: (bb, 0, 0, i)),
        pl.BlockSpec((1, MLA_HEADS, tm, MLA_QK), lambda bb, i: (bb, 0, i, 0)),
        pl.BlockSpec((1, MLA_HEADS, 1, V_ROWS, tm), lambda bb, i: (bb, 0, i, 0, 0)),
        pl.BlockSpec((1, DIFF_HEADS, LANE, tm), lambda bb, i: (bb, 0, 0, i)),
        pl.BlockSpec((1, DIFF_HEADS, LANE, tm), lambda bb, i: (bb, 0, 0, i)),
        pl.BlockSpec((1, DIFF_HEADS, tm, LANE), lambda bb, i: (bb, 0, i, 0)),
        pl.BlockSpec((1, DIFF_HEADS, tm, LANE), lambda bb, i: (bb, 0, i, 0)),
        pl.BlockSpec((1, DIFF_HEADS, 1, V_ROWS, tm), lambda bb, i: (bb, 0, i, 0, 0)),
    )
    return pl.pallas_call(
        _inproj_kernel,
        grid=(b, s // tm),
        in_specs=[
            pl.BlockSpec((1, tm, D_MODEL), lambda bb, i: (bb, i, 0)),
            const((D_MODEL, W_IN_COLS)),
            const((1, MLA_Q_RANK)),
            const((MLA_Q_RANK, MLA_HEADS * MLA_QK)),
            const((1, MLA_KV_RANK)),
            const((MLA_KV_RANK, MLA_HEADS * MLA_KV)),
            pl.BlockSpec((tm, LANE), lambda bb, i: (i, 0)),
        ],
        out_specs=out_specs,
        out_shape=out_shape,
        compiler_params=pltpu.CompilerParams(
            dimension_semantics=("parallel", "parallel"), vmem_limit_bytes=VMEM_LIMIT),
        name="inproj",
    )(x, win, qg, wq, kvg, wkv, rope)


def _qk_stage(k_tile, q_tile, s_ref, mb_ref, corr=None):
    s = _dot(k_tile, q_tile)
    if corr is not None:
        s = s - corr
    s_ref[...] = s
    mb_ref[...] = jnp.max(s, axis=0, keepdims=True)


def _sm_stage(s_ref, mb_ref, p_ref, m_ref):
    m_old = m_ref[...]
    m_new = jnp.maximum(m_old, mb_ref[...])
    p_ref[...] = jnp.exp2(s_ref[...] - m_new).astype(p_ref.dtype)
    m_ref[...] = m_new
    return jnp.exp2(m_old - m_new)


def _pv_stage(vt_tiles, p_ref, acc_ref, alpha):
    acc = acc_ref[...]
    for n, vt_tile in enumerate(vt_tiles):
        acc = acc + _dot(vt_tile, p_ref[n * KV_CHUNK:(n + 1) * KV_CHUNK, :])
    if alpha is None:
        return acc
    acc_ref[...] = alpha * acc


def _key_rows(c, tk):
    return pl.ds(pl.multiple_of(c * tk, tk), tk)


def _query_cols(tile, tq):
    return pl.ds(pl.multiple_of(tile * tq, tq), tq)


def _tile_pipeline(qi, nq, nk, *, setup, qk, sm, pv, restart, finish, lookahead=False):
    last_tile = qi == nq - 1
    first_head, last_head = _head_flags() if lookahead else (True, True)

    @pl.when(jnp.logical_and(first_head, qi == 0))
    def _():
        setup()
        restart()
        qk(0, 0, 0, False)
        qk(0, 1, 1, False)
        sm(0)

    def step(j, slot):
        qk(qi, j + 1, 1 - slot, False)
        pv(j - 1, 1 - slot, sm(slot))

    def pair(t, carry):
        step(2 * t + 1, 1)
        step(2 * t + 2, 0)
        return carry

    lax.fori_loop(0, nk // 2 - 1, pair, 0)

    def hand_over(tile, nxt):
        qk(tile, 0, 0, nxt)
        pv(nk - 2, 0, sm(1))
        qk(tile, 1, 1, nxt)
        restart()
        sm(0)
        finish(pv(nk - 1, 1, None))

    @pl.when(jnp.logical_not(last_tile))
    def _():
        hand_over(qi + 1, False)

    if lookahead:
        @pl.when(jnp.logical_and(last_tile, jnp.logical_not(last_head)))
        def _():
            hand_over(0, True)

    @pl.when(jnp.logical_and(last_tile, last_head))
    def _():
        pv(nk - 2, 0, sm(1))
        finish(pv(nk - 1, 1, None))


def _head_flags():
    b, h = pl.program_id(0), pl.program_id(1)
    first = jnp.logical_and(b == 0, h == 0)
    last = jnp.logical_and(b == pl.num_programs(0) - 1, h == pl.num_programs(1) - 1)
    return first, last


def _next_head(bb, h, n_heads, n_batch):
    flat = jnp.minimum(bb * n_heads + h + 1, n_batch * n_heads - 1)
    return flat // n_heads, flat % n_heads


def _mla_kernel(qt_ref, k_ref, vt_ref, qt_nx_ref, k_nx_ref, o_ref, s0, s1, p0, p1, mb0, mb1, m_sc, acc_sc):
    qi = pl.program_id(2)
    tk, tq = s0.shape
    sub = tk // KV_CHUNK
    nk = vt_ref.shape[2] // sub
    s_bufs, p_bufs, mb_bufs = (s0, s1), (p0, p1), (mb0, mb1)
    halves = [pl.ds(i * (tq // 2), tq // 2) for i in range(2)]

    def setup():
        acc_sc[...] = jnp.zeros_like(acc_sc)

    def qk(tile, j, slot, nxt):
        q_src, k_src = (qt_nx_ref, k_nx_ref) if nxt else (qt_ref, k_ref)
        for i, hv in enumerate(halves):
            cols = pl.ds(pl.multiple_of(tile * tq + i * (tq // 2), tq // 2), tq // 2)
            _qk_stage(k_src[0, 0, _key_rows(j, tk), :], q_src[0, 0, :, cols],
                      s_bufs[slot].at[:, hv], mb_bufs[slot].at[:, hv])

    def sm(slot):
        return [_sm_stage(s_bufs[slot].at[:, hv], mb_bufs[slot].at[:, hv], p_bufs[slot].at[:, hv],
                          m_sc.at[:, hv]) for hv in halves]

    def pv(j, slot, alphas):
        vt = [vt_ref[0, 0, j * sub + n] for n in range(sub)]
        return [_pv_stage(vt, p_bufs[slot].at[:, hv], acc_sc.at[:, hv], None if alphas is None else alphas[i])
                for i, hv in enumerate(halves)]

    def restart():
        m_sc[...] = jnp.full_like(m_sc, NEG_INF)

    def finish(totals):
        acc = jnp.concatenate(totals, axis=1)
        o = acc[:MLA_V] / acc[MLA_V:MLA_V + 1]
        o_ref[0] = o.T.astype(o_ref.dtype)
        acc_sc[...] = jnp.zeros_like(acc_sc)

    _tile_pipeline(qi, qt_ref.shape[3] // tq, nk, setup=setup, qk=qk, sm=sm, pv=pv, restart=restart, finish=finish,
                   lookahead=True)


def _mla_attention(qt, k, vt):
    b, hds, _, s = qt.shape
    nc = vt.shape[2]
    tq, tk = MLA_Q_TILE, MLA_K_TILE
    assert tk % KV_CHUNK == 0 and s % (2 * tk) == 0 and s % tq == 0
    f32, bf = jnp.float32, jnp.bfloat16
    return pl.pallas_call(
        _mla_kernel,
        grid=(b, hds, s // tq),
        in_specs=[
            pl.BlockSpec((1, 1, MLA_QK, s), lambda bb, h, i: (bb, h, 0, 0)),
            pl.BlockSpec((1, 1, s, MLA_QK), lambda bb, h, i: (bb, h, 0, 0)),
            pl.BlockSpec((1, 1, nc, V_ROWS, KV_CHUNK), lambda bb, h, i: (bb, h, 0, 0, 0)),
            pl.BlockSpec((1, 1, MLA_QK, tq), lambda bb, h, i: (*_next_head(bb, h, hds, b), 0, 0)),
            pl.BlockSpec((1, 1, 2 * tk, MLA_QK), lambda bb, h, i: (*_next_head(bb, h, hds, b), 0, 0)),
        ],
        out_specs=pl.BlockSpec((1, tq, MLA_V), lambda bb, h, i: (bb, i, h)),
        out_shape=jax.ShapeDtypeStruct((b, s, hds * MLA_V), bf),
        scratch_shapes=[
            pltpu.VMEM((tk, tq), f32), pltpu.VMEM((tk, tq), f32),
            pltpu.VMEM((tk, tq), bf), pltpu.VMEM((tk, tq), bf),
            pltpu.VMEM((1, tq), f32), pltpu.VMEM((1, tq), f32),
            pltpu.VMEM((1, tq), f32),
            pltpu.VMEM((V_ROWS, tq), f32),
        ],
        compiler_params=pltpu.CompilerParams(
            dimension_semantics=("arbitrary", "arbitrary", "arbitrary"), vmem_limit_bytes=VMEM_LIMIT),
        name="mla_attn",
    )(qt, k, vt, qt, k)


def _diff_kernel(q1t_ref, q2t_ref, k1_ref, k2_ref, vt_ref, lam_ref, linit_ref, g_ref, o_ref,
                 qv_sc, corr_sc, s_sc, p_sc, mb_sc, m_sc, acc_sc):
    h = pl.program_id(1)
    qi = pl.program_id(2)
    tq = o_ref.shape[1]
    sub = tq // KV_CHUNK
    nk = vt_ref.shape[2] // sub
    k_refs = (k1_ref, k2_ref)
    q_refs = (q1t_ref, q2t_ref)

    def setup():
        jj = lax.broadcasted_iota(jnp.int32, (tq, tq), 0)
        ii = lax.broadcasted_iota(jnp.int32, (tq, tq), 1)
        expo = jnp.full((1, tq), 127 + 1, jnp.int32) - 2 * (h + 1)
        slope2 = lax.bitcast_convert_type(expo << 23, jnp.float32)
        corr_sc[...] = (slope2 * LOG2E) * jnp.maximum(jj - ii, 0).astype(jnp.float32)
        acc_sc[...] = jnp.zeros_like(acc_sc)

    def chunk(tile, j):
        return lax.rem(tile + j, nk)

    def load_queries(tile):
        row = lax.broadcasted_iota(jnp.int32, (LANE, tq), 0)
        for mi in range(2):
            qq = q_refs[mi][0, 0, :, _query_cols(tile, tq)]
            qv_sc[mi] = qq
            qv_sc[2 + mi] = jnp.where(row < DIFF_DIM, qq, -qq)

    def qk(tile, j, slot, nxt):
        assert not nxt
        diagonal = isinstance(j, int) and j == 0
        if diagonal:
            load_queries(tile)
        c = chunk(tile, j)
        variant = 0 if diagonal else jnp.where(c > tile, 2, 0)
        for mi in range(2):
            _qk_stage(k_refs[mi][0, 0, _key_rows(c, tq), :], qv_sc[variant + mi],
                      s_sc.at[2 * slot + mi], mb_sc.at[2 * slot + mi],
                      corr_sc[...] if diagonal else None)

    def sm(slot):
        return [_sm_stage(s_sc.at[2 * slot + mi], mb_sc.at[2 * slot + mi], p_sc.at[2 * slot + mi],
                          m_sc.at[mi]) for mi in range(2)]

    def pv(j, slot, alphas):
        c = chunk(qi, j)
        vt = [vt_ref[0, 0, c * sub + n] for n in range(sub)]
        return [_pv_stage(vt, p_sc.at[2 * slot + mi], acc_sc.at[mi], None if alphas is None else alphas[mi])
                for mi in range(2)]

    def restart():
        m_sc[...] = jnp.full_like(m_sc, NEG_INF)

    def finish(totals):
        lamp = lam_ref[...]
        linit = linit_ref[...]
        lam = (jnp.exp(jnp.sum(lamp[0:1] * lamp[1:2], axis=-1, keepdims=True))
               - jnp.exp(jnp.sum(lamp[2:3] * lamp[3:4], axis=-1, keepdims=True)) + linit)
        nv = 2 * DIFF_DIM
        a1, a2 = totals
        o = a1[:nv] * (1.0 / a1[nv:nv + 1]) - a2[:nv] * (lam / a2[nv:nv + 1])
        o = o * lax.rsqrt(jnp.mean(o * o, axis=0, keepdims=True) + NORM_EPS)
        o_ref[0] = (o.T * (g_ref[...] * (1.0 - linit))).astype(o_ref.dtype)
        acc_sc[...] = jnp.zeros_like(acc_sc)

    _tile_pipeline(qi, q1t_ref.shape[3] // tq, nk, setup=setup, qk=qk, sm=sm, pv=pv, restart=restart, finish=finish)


def _diff_attention(layer, q1t, q2t, k1, k2, vt, lamp, linit, subln_g):
    b, hds, _, s = q1t.shape
    nc = vt.shape[2]
    t = DIFF_TILE
    assert t % KV_CHUNK == 0 and s % (2 * t) == 0
    f32, bf = jnp.float32, jnp.bfloat16
    qspec = pl.BlockSpec((1, 1, LANE, s), lambda bb, h, i: (bb, h, 0, 0))
    kspec = pl.BlockSpec((1, 1, s, LANE), lambda bb, h, i: (bb, h, 0, 0))
    const = lambda shape: _layer_block(shape, layer)
    return pl.pallas_call(
        _diff_kernel,
        grid=(b, hds, s // t),
        in_specs=[
            qspec, qspec, kspec, kspec,
            pl.BlockSpec((1, 1, nc, V_ROWS, KV_CHUNK), lambda bb, h, i: (bb, h, 0, 0, 0)),
            const((4, DIFF_DIM)), const((1, 1)), const((1, 2 * DIFF_DIM)),
        ],
        out_specs=pl.BlockSpec((1, t, 2 * DIFF_DIM), lambda bb, h, i: (bb, i, h)),
        out_shape=jax.ShapeDtypeStruct((b, s, hds * 2 * DIFF_DIM), bf),
        scratch_shapes=[
            pltpu.VMEM((4, LANE, t), bf),
            pltpu.VMEM((t, t), f32),
            pltpu.VMEM((4, t, t), f32),
            pltpu.VMEM((4, t, t), bf),
            pltpu.VMEM((4, 1, t), f32),
            pltpu.VMEM((2, 1, t), f32),
            pltpu.VMEM((2, V_ROWS, t), f32),
        ],
        compiler_params=pltpu.CompilerParams(
            dimension_semantics=("parallel", "parallel", "arbitrary"), vmem_limit_bytes=VMEM_LIMIT),
        name="diff_attn",
    )(q1t, q2t, k1, k2, vt, lamp, linit, subln_g)


def _swap_halves(w):
    half = w.shape[-1] // 2
    return jnp.concatenate([w[..., half:], w[..., :half]], axis=-1)


def _prep_w_in(w_in):
    w = w_in.astype(jnp.bfloat16)
    rope_end = MLA_Q_RANK + MLA_KV_RANK + MLA_ROPE
    k_rope = w[..., rope_end - MLA_ROPE:rope_end]
    return jnp.concatenate([w[..., :rope_end], _swap_halves(k_rope), w[..., rope_end:]], axis=-1)


def _prep_w_q_up(w_q_up):
    l, r, _ = w_q_up.shape
    w = w_q_up.astype(jnp.bfloat16).reshape(l, r, MLA_HEADS, MLA_NOPE + MLA_ROPE)
    w = jnp.concatenate([w, _swap_halves(w[..., MLA_NOPE:])], axis=-1)
    return w.reshape(l, r, MLA_HEADS * MLA_QK)


def _rope_table(s):
    inv_freq = np.float32(ROPE_THETA) ** (-np.arange(0, MLA_ROPE, 2, dtype=np.float32) / np.float32(MLA_ROPE))
    ang = np.arange(s, dtype=np.float32)[:, None] * inv_freq[None, :]
    cos, sin = np.cos(ang), np.sin(ang)
    return jnp.asarray(np.concatenate([cos, cos, -sin, sin], axis=-1), jnp.float32)


def kernel(x, ffn1_w_gate, ffn1_w_up, ffn1_w_down, ln1_g, ln1_b, w_in, q_norm_g, w_q_up, kv_norm_g, w_kv_up, diff_lambda_q1, diff_lambda_k1, diff_lambda_q2, diff_lambda_k2, diff_subln_g, w_out, ln2_g, ln2_b, ffn2_w_gate, ffn2_w_up, ffn2_w_down, ln3_g, ln3_b):
    b, s, d = x.shape
    bf = jnp.bfloat16
    rope = _rope_table(s)
    w_in_p = _prep_w_in(w_in)
    w_q_p = _prep_w_q_up(w_q_up)
    w_kv_p = w_kv_up.astype(bf)
    w_out_p = w_out.astype(bf)
    f1 = (ffn1_w_gate.astype(bf), ffn1_w_up.astype(bf), ffn1_w_down.astype(bf))
    f2 = (ffn2_w_gate.astype(bf), ffn2_w_up.astype(bf), ffn2_w_down.astype(bf))
    lamp = jnp.stack([diff_lambda_q1, diff_lambda_k1, diff_lambda_q2, diff_lambda_k2], axis=1)
    linit = jnp.asarray([[[0.8 - 0.6 * math.exp(-0.3 * l)]] for l in range(DEPTH)], jnp.float32)
    rows = lambda v: v[:, None, :]

    xf = x.reshape(b * s, d)
    for l in range(DEPTH):
        xf = _ffn_ln(l, xf, *f1, rows(ln1_g), rows(ln1_b))
        qt, km, vtm, q1t, q2t, k1, k2, vtd = _inproj(
            l, xf.reshape(b, s, d), w_in_p, rows(q_norm_g), w_q_p, rows(kv_norm_g), w_kv_p, rope)
        o_mla = _mla_attention(qt, km, vtm)
        o_diff = _diff_attention(l, q1t, q2t, k1, k2, vtd, lamp, linit, rows(diff_subln_g))
        xf = _mix_ffn_ln(l, xf, o_mla.reshape(b * s, -1), o_diff.reshape(b * s, -1), w_out_p,
                         rows(ln2_g), rows(ln2_b), *f2, rows(ln3_g), rows(ln3_b))
    return xf.reshape(b, s, d)
```
